```python
import jax, jax.numpy as jnp
from jax import lax
import numpy as np

D_MODEL = 1024
BATCH = 16
SEQ = 2048
DEPTH = 1
DEC_BATCH = 32
DEC_SEQ = 32
PAST_LEN = 1024

CHUNK = 64
N_META = 16
D_RG = 1024
N_RG_HEADS = 16
RG_HEAD_DIM = D_RG // N_RG_HEADS
RG_CONV = 4
RG_C = 8.0
D_CV = 1024
CV_CONV = 31
D_MIX = D_RG + D_CV
D_IN = 2 * D_RG + 3 * D_CV
EPS = 1e-6

kernel_name = 'hymba_rglru_conformer_stream_step'


def rms_norm(x, g):
    xf = x.astype(jnp.float32)
    y = xf * lax.rsqrt(jnp.mean(xf * xf, axis=-1, keepdims=True) + EPS)
    return (y * g.astype(jnp.float32)).astype(x.dtype)


def layer_norm(x, g, b):
    xf = x.astype(jnp.float32)
    mu = jnp.mean(xf, axis=-1, keepdims=True)
    xc = xf - mu
    var = jnp.mean(xc * xc, axis=-1, keepdims=True)
    y = xc * lax.rsqrt(var + EPS) * g.astype(jnp.float32) + b.astype(jnp.float32)
    return y.astype(x.dtype)


def causal_dwconv(x, buf, w, b):
    width = w.shape[0]
    xp = jnp.concatenate([buf.astype(x.dtype), x], axis=1)
    y = lax.conv_general_dilated(
        xp, w[:, None, :].astype(x.dtype), window_strides=(1,), padding='VALID',
        dimension_numbers=('NWC', 'WIO', 'NWC'), feature_group_count=x.shape[-1])
    return y + b.astype(x.dtype), xp[:, xp.shape[1] - (width - 1):]


def rg_lru(x, h0, w_a, b_a, w_x, b_x, lam):
    bsz, t_len, _ = x.shape
    xf = x.astype(jnp.float32)
    xh = xf.reshape(bsz, t_len, N_RG_HEADS, RG_HEAD_DIM)
    r = jax.nn.sigmoid(jnp.einsum('bthi,hij->bthj', xh, w_a.astype(jnp.float32)).reshape(bsz, t_len, D_RG)
                       + b_a.astype(jnp.float32))
    i = jax.nn.sigmoid(jnp.einsum('bthi,hij->bthj', xh, w_x.astype(jnp.float32)).reshape(bsz, t_len, D_RG)
                       + b_x.astype(jnp.float32))
    log_a = -RG_C * r * jax.nn.softplus(-lam.astype(jnp.float32))
    a = jnp.exp(log_a)
    bx = jnp.sqrt(-jnp.expm1(2.0 * log_a)) * (i * xf)

    def step(h, ab):
        a_t, b_t = ab
        h = a_t * h + b_t
        return h, h

    h_last, hs = lax.scan(step, h0.astype(jnp.float32), (a.swapaxes(0, 1), bx.swapaxes(0, 1)))
    return hs.swapaxes(0, 1).astype(x.dtype), h_last.astype(h0.dtype)


def hybrid_layer(x, rg_buf, rg_h, cv_buf, pre_g, w_in, rg_cw, rg_cb, rg_wa, rg_ba, rg_wx, rg_bx, rg_lam,
                 cv_cw, cv_cb, cv_lg, cv_lb, cv_w2, cv_b2, w_out, post_g):
    xn = rms_norm(x, pre_g)
    u = jnp.einsum('btd,de->bte', xn, w_in.astype(x.dtype))
    xr, gr, cv_v, cv_g, gc = jnp.split(u, [D_RG, 2 * D_RG, 2 * D_RG + D_CV, 2 * D_RG + 2 * D_CV], axis=-1)
    xr, rg_buf_new = causal_dwconv(xr, rg_buf, rg_cw, rg_cb)
    hr, rg_h_new = rg_lru(xr, rg_h, rg_wa, rg_ba, rg_wx, rg_bx, rg_lam)
    y_r = hr * jax.nn.silu(gr)
    v = cv_v * jax.nn.sigmoid(cv_g)
    v, cv_buf_new = causal_dwconv(v, cv_buf, cv_cw, cv_cb)
    v = jax.nn.silu(layer_norm(v, cv_lg, cv_lb))
    y_c = (jnp.einsum('btc,ce->bte', v, cv_w2.astype(x.dtype)) + cv_b2.astype(x.dtype)) * jax.nn.silu(gc)
    y = jnp.einsum('btm,md->btd', jnp.concatenate([y_r, y_c], axis=-1), w_out.astype(x.dtype))
    return x + rms_norm(y, post_g), rg_buf_new, rg_h_new, cv_buf_new


def setup_inputs(seed: int = 0) -> dict:
    key = jax.random.key(seed)
    ks = jax.random.split(key, 24)
    f32 = jnp.float32
    nrm = lambda k, s, sc: jax.random.normal(k, s, f32) * sc
    u = jax.random.uniform(ks[12], (DEPTH, D_RG), f32, 0.9, 0.999)
    a0 = u ** (1.0 / RG_C)
    lam = jnp.log(a0) - jnp.log1p(-a0)
    return {
        'x_prompt': nrm(ks[0], (BATCH, SEQ, D_MODEL), 1.0),
        'x_sample': nrm(ks[1], (DEC_BATCH, DEC_SEQ, D_MODEL), 1.0),
        'state_rg_h': nrm(ks[2], (DEPTH, DEC_BATCH, D_RG), 0.5),
        'state_rg_conv': nrm(ks[3], (DEPTH, DEC_BATCH, RG_CONV - 1, D_RG), 1.0),
        'state_cv_conv': nrm(ks[4], (DEPTH, DEC_BATCH, CV_CONV - 1, D_CV), 0.5),
        'meta_tokens': nrm(ks[5], (N_META, D_MODEL), 1.0),
        'pre_norm_g': 1.0 + nrm(ks[6], (DEPTH, D_MODEL), 0.02),
        'w_in': nrm(ks[7], (DEPTH, D_MODEL, D_IN), D_MODEL ** -0.5),
        'rg_conv_w': nrm(ks[8], (DEPTH, RG_CONV, D_RG), RG_CONV ** -0.5),
        'rg_conv_b': nrm(ks[9], (DEPTH, D_RG), 0.02),
        'rg_wa': nrm(ks[10], (DEPTH, N_RG_HEADS, RG_HEAD_DIM, RG_HEAD_DIM), RG_HEAD_DIM ** -0.5),
        'rg_ba': nrm(ks[11], (DEPTH, D_RG), 0.02),
        'rg_wx': nrm(ks[13], (DEPTH, N_RG_HEADS, RG_HEAD_DIM, RG_HEAD_DIM), RG_HEAD_DIM ** -0.5),
        'rg_bx': nrm(ks[14], (DEPTH, D_RG), 0.02),
        'rg_lambda': lam,
        'cv_conv_w': nrm(ks[15], (DEPTH, CV_CONV, D_CV), CV_CONV ** -0.5),
        'cv_conv_b': nrm(ks[16], (DEPTH, D_CV), 0.02),
        'cv_ln_g': 1.0 + nrm(ks[17], (DEPTH, D_CV), 0.02),
        'cv_ln_b': nrm(ks[18], (DEPTH, D_CV), 0.02),
        'cv_w_pw2': nrm(ks[19], (DEPTH, D_CV, D_CV), D_CV ** -0.5),
        'cv_b_pw2': nrm(ks[20], (DEPTH, D_CV), 0.02),
        'w_out': nrm(ks[21], (DEPTH, D_MIX, D_MODEL), D_MIX ** -0.5),
        'post_norm_g': 1.0 + nrm(ks[22], (DEPTH, D_MODEL), 0.02),
    }


def reference(x_prompt, x_sample, state_rg_h, state_rg_conv, state_cv_conv, meta_tokens, pre_norm_g, w_in,
              rg_conv_w, rg_conv_b, rg_wa, rg_ba, rg_wx, rg_bx, rg_lambda, cv_conv_w, cv_conv_b, cv_ln_g,
              cv_ln_b, cv_w_pw2, cv_b_pw2, w_out, post_norm_g):
    bsz = x_prompt.shape[0]
    dt = x_prompt.dtype
    hp = jnp.concatenate([jnp.broadcast_to(meta_tokens.astype(dt)[None], (bsz, N_META, D_MODEL)), x_prompt], axis=1)
    hs = x_sample
    p_rgh, p_rgc, p_cvc, s_rgh, s_rgc, s_cvc = [], [], [], [], [], []
    for l in range(DEPTH):
        w = (pre_norm_g[l], w_in[l], rg_conv_w[l], rg_conv_b[l], rg_wa[l], rg_ba[l], rg_wx[l], rg_bx[l],
             rg_lambda[l], cv_conv_w[l], cv_conv_b[l], cv_ln_g[l], cv_ln_b[l], cv_w_pw2[l], cv_b_pw2[l],
             w_out[l], post_norm_g[l])
        hp, rgc, rgh, cvc = hybrid_layer(
            hp, jnp.zeros((bsz, RG_CONV - 1, D_RG), dt), jnp.zeros((bsz, D_RG), dt),
            jnp.zeros((bsz, CV_CONV - 1, D_CV), dt), *w)
        p_rgh.append(rgh); p_rgc.append(rgc); p_cvc.append(cvc)
        hs, rgc, rgh, cvc = hybrid_layer(hs, state_rg_conv[l], state_rg_h[l], state_cv_conv[l], *w)
        s_rgh.append(rgh); s_rgc.append(rgc); s_cvc.append(cvc)
    y_prompt = hp[:, N_META:]
    y_sample = hs
    return (y_prompt, y_sample, jnp.stack(p_rgh), jnp.stack(p_rgc), jnp.stack(p_cvc),
            jnp.stack(s_rgh), jnp.stack(s_rgc), jnp.stack(s_cvc))
```

```python
import functools

import jax
import jax.numpy as jnp
from jax.experimental import pallas as pl
from jax.experimental.pallas import tpu as pltpu

D_MODEL = 1024
N_META = 16
D_RG = 1024
N_RG_HEADS = 16
RG_HEAD_DIM = D_RG // N_RG_HEADS
RG_CONV = 4
RG_C = 8.0
D_CV = 1024
CV_CONV = 31
D_MIX = D_RG + D_CV
D_IN = 2 * D_RG + 3 * D_CV
EPS = 1e-6

RG_HIST = RG_CONV - 1
CV_HIST = CV_CONV - 1
GATE_GROUP = 256
N_GATE_GROUPS = D_RG // GATE_GROUP
CHUNK_ROWS = 128
VMEM_LIMIT_BYTES = 56 * 1024 * 1024

V_PRE_G, V_RG_CB, V_BA, V_BX, V_LAM, V_CV_CB, V_LN_G, V_LN_B, V_B2, V_POST_G, V_RG_CW = range(11)
N_VEC_ROWS = 16


def _silu(x):
    return x * jax.nn.sigmoid(x)


def _layer_kernel(x_ref, h0_ref, rgb0_ref, cvb0_ref, vec_ref, cvw_ref, w_in_ref, wg_ref, pw2_ref, w_out_ref,
                  y_ref, h_out_ref, rgb_out_ref, cvb_out_ref,
                  xr_ext, v_ext, h_scr, *, batch, block_t, chunk_t, n_steps):
    B = batch
    R = chunk_t * B
    n_chunks = block_t // chunk_t
    step = pl.program_id(0)

    @pl.when(step == 0)
    def _init():
        xr_ext[0:RG_HIST * B, :] = rgb0_ref[...]
        v_ext[0:CV_HIST * B, :] = cvb0_ref[...]
        h_scr[...] = h0_ref[...]

    def vec(row):
        return vec_ref[row:row + 1, :]

    lam = vec(V_LAM)
    decay = RG_C * (jnp.maximum(-lam, 0.0) + jnp.log1p(jnp.exp(-jnp.abs(lam))))

    def chunk(c, carry):
        r0 = pl.multiple_of(c * R, R)
        x = x_ref[pl.ds(r0, R), :]
        xn = x * jax.lax.rsqrt(jnp.mean(x * x, axis=-1, keepdims=True) + EPS) * vec(V_PRE_G)
        xn = xn.astype(jnp.bfloat16)

        def in_proj(j):
            return jnp.dot(xn, w_in_ref[:, j * D_RG:(j + 1) * D_RG], preferred_element_type=jnp.float32)

        xr = in_proj(0)
        xr_ext[pl.ds(pl.multiple_of(RG_HIST * B + r0, 8), R), :] = xr
        xc = vec(V_RG_CB)
        for k in range(RG_CONV):
            tap = xr_ext[pl.ds(pl.multiple_of(k * B + r0, 8), R), :]
            xc = xc + vec_ref[V_RG_CW + k:V_RG_CW + k + 1, :] * tap
        xcb = xc.astype(jnp.bfloat16)
        zs = [jnp.dot(xcb[:, g * GATE_GROUP:(g + 1) * GATE_GROUP], wg_ref[g],
                      preferred_element_type=jnp.float32) for g in range(N_GATE_GROUPS)]
        z_a = jnp.concatenate([z[:, :GATE_GROUP] for z in zs], axis=-1)
        z_x = jnp.concatenate([z[:, GATE_GROUP:] for z in zs], axis=-1)
        r_gate = jax.nn.sigmoid(z_a + vec(V_BA))
        i_gate = jax.nn.sigmoid(z_x + vec(V_BX))
        neg_log_a = r_gate * decay
        a = jnp.exp(-neg_log_a)
        one_minus_a2 = jnp.tanh(neg_log_a) * (a * a + 1.0)
        bx = jnp.sqrt(one_minus_a2) * (i_gate * xc)
        h = h_scr[...]
        hs = []
        for s in range(chunk_t):
            h = a[s * B:(s + 1) * B, :] * h + bx[s * B:(s + 1) * B, :]
            hs.append(h)
        h_scr[...] = h
        hr = jnp.concatenate(hs, axis=0)
        y_r = (hr * _silu(in_proj(1))).astype(jnp.bfloat16)

        v = in_proj(2) * jax.nn.sigmoid(in_proj(3))
        v_ext[pl.ds(pl.multiple_of(CV_HIST * B + r0, 8), R), :] = v
        cv = vec(V_CV_CB)
        for k in range(CV_CONV):
            tap = v_ext[pl.ds(pl.multiple_of(k * B + r0, 8), R), :]
            cv = cv + cvw_ref[k:k + 1, :] * tap
        mu = jnp.mean(cv, axis=-1, keepdims=True)
        cvc = cv - mu
        var = jnp.mean(cvc * cvc, axis=-1, keepdims=True)
        ln = cvc * jax.lax.rsqrt(var + EPS) * vec(V_LN_G) + vec(V_LN_B)
        s_act = _silu(ln).astype(jnp.bfloat16)
        z_c = jnp.dot(s_act, pw2_ref[...], preferred_element_type=jnp.float32) + vec(V_B2)
        y_c = (z_c * _silu(in_proj(4))).astype(jnp.bfloat16)

        y = jnp.dot(jnp.concatenate([y_r, y_c], axis=-1), w_out_ref[...], preferred_element_type=jnp.float32)
        yn = y * jax.lax.rsqrt(jnp.mean(y * y, axis=-1, keepdims=True) + EPS) * vec(V_POST_G)
        y_ref[pl.ds(r0, R), :] = x + yn
        return carry

    jax.lax.fori_loop(0, n_chunks, chunk, 0)

    blk = block_t * B

    @pl.when(step == n_steps - 1)
    def _emit_state():
        h_out_ref[...] = h_scr[...]
        rgb_out_ref[...] = xr_ext[blk:blk + RG_HIST * B, :]
        cvb_out_ref[...] = v_ext[blk:blk + CV_HIST * B, :]

    if n_steps > 1:
        @pl.when(step < n_steps - 1)
        def _carry_hist():
            xr_ext[0:RG_HIST * B, :] = xr_ext[blk:blk + RG_HIST * B, :]
            v_ext[0:CV_HIST * B, :] = v_ext[blk:blk + CV_HIST * B, :]


def _stream_layer(x_rows, h0, rgb0, cvb0, weights, *, batch, n_t, block_t):
    B = batch
    assert B % 8 == 0 and CHUNK_ROWS % B == 0
    chunk_t = CHUNK_ROWS // B
    assert n_t % block_t == 0 and block_t % chunk_t == 0
    n_steps = n_t // block_t
    assert n_steps == 1 or block_t >= CV_HIST
    blk = block_t * B
    vecs, cvw, w_in, wg, pw2, w_out = weights

    def full(a):
        return pl.BlockSpec(a.shape, lambda i: (0,) * a.ndim)

    kern = functools.partial(_layer_kernel, batch=B, block_t=block_t, chunk_t=chunk_t, n_steps=n_steps)
    f32 = jnp.float32
    return pl.pallas_call(
        kern,
        grid=(n_steps,),
        in_specs=[pl.BlockSpec((blk, D_MODEL), lambda i: (i, 0)),
                  full(h0), full(rgb0), full(cvb0), full(vecs), full(cvw), full(w_in), full(wg), full(pw2),
                  full(w_out)],
        out_specs=[pl.BlockSpec((blk, D_MODEL), lambda i: (i, 0)),
                   pl.BlockSpec((B, D_RG), lambda i: (0, 0)),
                   pl.BlockSpec((RG_HIST * B, D_RG), lambda i: (0, 0)),
                   pl.BlockSpec((CV_HIST * B, D_CV), lambda i: (0, 0))],
        out_shape=[jax.ShapeDtypeStruct((n_t * B, D_MODEL), f32),
                   jax.ShapeDtypeStruct((B, D_RG), f32),
                   jax.ShapeDtypeStruct((RG_HIST * B, D_RG), f32),
                   jax.ShapeDtypeStruct((CV_HIST * B, D_CV), f32)],
        scratch_shapes=[pltpu.VMEM(((RG_HIST + block_t) * B, D_RG), f32),
                        pltpu.VMEM(((CV_HIST + block_t) * B, D_CV), f32),
                        pltpu.VMEM((B, D_RG), f32)],
        compiler_params=pltpu.CompilerParams(dimension_semantics=("arbitrary",),
                                             vmem_limit_bytes=VMEM_LIMIT_BYTES),
    )(x_rows, h0, rgb0, cvb0, vecs, cvw, w_in, wg, pw2, w_out)


def _block_diag_gates(wa, wx):
    hpg = GATE_GROUP // RG_HEAD_DIM
    eye = jnp.eye(hpg, dtype=wa.dtype)

    def bd(w):
        w4 = w.reshape(N_GATE_GROUPS, hpg, RG_HEAD_DIM, RG_HEAD_DIM)
        return jnp.einsum('ghij,hk->ghikj', w4, eye).reshape(N_GATE_GROUPS, GATE_GROUP, GATE_GROUP)

    return jnp.concatenate([bd(wa), bd(wx)], axis=-1).astype(jnp.bfloat16)


def _to_rows(x):
    b, t, c = x.shape
    return x.transpose(1, 0, 2).reshape(t * b, c)


def _from_rows(x, b):
    tb, c = x.shape
    return x.reshape(tb // b, b, c).transpose(1, 0, 2)


def kernel(x_prompt, x_sample, state_rg_h, state_rg_conv, state_cv_conv, meta_tokens, pre_norm_g, w_in,
           rg_conv_w, rg_conv_b, rg_wa, rg_ba, rg_wx, rg_bx, rg_lambda, cv_conv_w, cv_conv_b, cv_ln_g,
           cv_ln_b, cv_w_pw2, cv_b_pw2, w_out, post_norm_g):
    assert w_in.shape[0] == 1, "single layer"
    f32 = jnp.float32
    bsz, seq, _ = x_prompt.shape
    dbsz, dseq, _ = x_sample.shape

    rows = [pre_norm_g[0], rg_conv_b[0], rg_ba[0], rg_bx[0], rg_lambda[0], cv_conv_b[0], cv_ln_g[0], cv_ln_b[0],
            cv_b_pw2[0], post_norm_g[0]]
    vecs = jnp.concatenate([jnp.stack(rows).astype(f32), rg_conv_w[0].astype(f32),
                            jnp.zeros((N_VEC_ROWS - len(rows) - RG_CONV, D_MODEL), f32)], axis=0)
    cvw = jnp.concatenate([cv_conv_w[0].astype(f32), jnp.zeros((1, D_CV), f32)], axis=0)
    weights = (vecs, cvw, w_in[0].astype(jnp.bfloat16), _block_diag_gates(rg_wa[0], rg_wx[0]),
               cv_w_pw2[0].astype(jnp.bfloat16), w_out[0].astype(jnp.bfloat16))

    mb = 8
    xm = jnp.broadcast_to(meta_tokens.astype(f32)[:, None, :], (N_META, mb, D_MODEL)).reshape(N_META * mb, D_MODEL)
    _, h_m, rgb_m, cvb_m = _stream_layer(
        xm, jnp.zeros((mb, D_RG), f32), jnp.zeros((RG_HIST * mb, D_RG), f32), jnp.zeros((CV_HIST * mb, D_CV), f32),
        weights, batch=mb, n_t=N_META, block_t=N_META)

    def rebatch(s, n_hist):
        s = s.reshape(n_hist, mb, s.shape[-1])[:, :1, :]
        return jnp.broadcast_to(s, (n_hist, bsz, s.shape[-1])).reshape(n_hist * bsz, s.shape[-1])

    yp, h_p, rgb_p, cvb_p = _stream_layer(
        _to_rows(x_prompt), rebatch(h_m, 1), rebatch(rgb_m, RG_HIST), rebatch(cvb_m, CV_HIST),
        weights, batch=bsz, n_t=seq, block_t=32)

    ys, h_s, rgb_s, cvb_s = _stream_layer(
        _to_rows(x_sample), state_rg_h[0], _to_rows(state_rg_conv[0]), _to_rows(state_cv_conv[0]),
        weights, batch=dbsz, n_t=dseq, block_t=dseq)

    return (_from_rows(yp, bsz), _from_rows(ys, dbsz),
            h_p[None], _from_rows(rgb_p, bsz)[None], _from_rows(cvb_p, bsz)[None],
            h_s[None], _from_rows(rgb_s, dbsz)[None], _from_rows(cvb_s, dbsz)[None])
```

```python
import functools

import jax
import jax.numpy as jnp
from jax.experimental import pallas as pl
from jax.experimental.pallas import tpu as pltpu

D_MODEL = 1024
N_META = 16
D_RG = 1024
N_RG_HEADS = 16
RG_HEAD_DIM = D_RG // N_RG_HEADS
RG_CONV = 4
RG_C = 8.0
D_CV = 1024
CV_CONV = 31
EPS = 1e-6

RG_HIST = RG_CONV - 1
CV_HIST = CV_CONV - 1
SUBLANES, LANES = 8, 128
MXU_TILE = 256
N_COL_TILES = D_RG // MXU_TILE
LANES_PER_TILE = MXU_TILE // LANES
N_LANE_GROUPS = D_CV // LANES
N_IN_SECTIONS = 5
SEC_XR, SEC_GR, SEC_CVV, SEC_CVG, SEC_GC = range(N_IN_SECTIONS)
CHUNK_ROWS = 128
VMEM_LIMIT_BYTES = 56 * 1024 * 1024

V_PRE_G, V_RG_CB, V_BA, V_BX, V_LAM, V_LN_G, V_LN_B, V_B2, V_POST_G, V_RG_CW = range(10)
N_VEC_ROWS = 16
CVW_ROWS = 32


def _silu(x):
    return x * jax.nn.sigmoid(x)


def _bf16_rows(w_u32):
    return pltpu.bitcast(w_u32, jnp.bfloat16)


def _pack_rows(w):
    *lead, k, n = w.shape
    bits = jax.lax.bitcast_convert_type(w.astype(jnp.bfloat16), jnp.uint16).astype(jnp.uint32)
    bits = bits.reshape(*lead, k // 2, 2, n)
    return bits[..., 0, :] | (bits[..., 1, :] << 16)


def _decay(vec_ref):
    lam = vec_ref[V_LAM:V_LAM + 1, :]
    return RG_C * (jnp.maximum(-lam, 0.0) + jnp.log1p(jnp.exp(-jnp.abs(lam))))


def _row_sum(parts):
    s = parts[0]
    for p in parts[1:]:
        s = s + p
    return jnp.sum(s, axis=-1, keepdims=True)


def _chunk_body(r0, *, R, B, decay, x_ref, y_ref, vec_ref, cvw_ref, w_in_ref, wg_ref, pw2_ref, w_out_ref,
                xr_ext, h_scr, v_ext):
    chunk_t = R // B
    rows = pl.ds(r0, R)
    n_tiles = R // SUBLANES
    step = B // SUBLANES

    def vec(row, cols):
        return vec_ref[row:row + 1, cols]

    x = x_ref[rows, :]
    xn = x * jax.lax.rsqrt(jnp.mean(x * x, axis=-1, keepdims=True) + EPS) * vec_ref[V_PRE_G:V_PRE_G + 1, :]
    xn = xn.astype(jnp.bfloat16)

    def in_proj(sec):
        return jnp.dot(xn, _bf16_rows(w_in_ref[:, sec * D_RG:(sec + 1) * D_RG]), preferred_element_type=jnp.float32)

    v = in_proj(SEC_CVV) * jax.nn.sigmoid(in_proj(SEC_CVG))
    for j in range(N_LANE_GROUPS):
        v_ext[j, pl.ds(pl.multiple_of(CV_HIST * B + r0, 8), R), :] = v[:, j * LANES:(j + 1) * LANES]
    cv = []
    for j in range(N_LANE_GROUPS):
        acc = [jnp.broadcast_to(cvw_ref[j, CV_CONV:CV_CONV + 1, :], (SUBLANES, LANES))] * n_tiles
        for k in range(CV_CONV):
            w = jnp.broadcast_to(cvw_ref[j, k:k + 1, :], (SUBLANES, LANES))
            for i in range(n_tiles):
                row = pl.multiple_of(r0 + (i + step * k) * SUBLANES, SUBLANES)
                acc[i] = acc[i] + w * v_ext[j, pl.ds(row, SUBLANES), :]
        cv.append(jnp.concatenate(acc, axis=0))
    inv_d = 1.0 / D_CV
    mu = _row_sum(cv) * inv_d
    cvc = [p - mu for p in cv]
    var = _row_sum([p * p for p in cvc]) * inv_d
    rs = jax.lax.rsqrt(var + EPS)
    s_act = []
    for j in range(N_LANE_GROUPS):
        lcols = slice(j * LANES, (j + 1) * LANES)
        ln = cvc[j] * rs * vec(V_LN_G, lcols) + vec(V_LN_B, lcols)
        s_act.append(_silu(ln).astype(jnp.bfloat16))
    s_act = jnp.concatenate(s_act, axis=-1)

    xr = in_proj(SEC_XR)
    for j in range(N_LANE_GROUPS):
        xr_ext[j, pl.ds(pl.multiple_of(RG_HIST * B + r0, 8), R), :] = xr[:, j * LANES:(j + 1) * LANES]
    sgr = _silu(in_proj(SEC_GR))
    y_r = []
    for g in range(N_COL_TILES):
        cols = slice(g * MXU_TILE, (g + 1) * MXU_TILE)
        halves = []
        for half in range(LANES_PER_TILE):
            j = g * LANES_PER_TILE + half
            lcols = slice(j * LANES, (j + 1) * LANES)
            acc = vec(V_RG_CB, lcols)
            for k in range(RG_CONV):
                tap = xr_ext[j, pl.ds(pl.multiple_of(k * B + r0, 8), R), :]
                acc = acc + vec(V_RG_CW + k, lcols) * tap
            halves.append(acc)
        xc = jnp.concatenate(halves, axis=-1)
        z = jnp.dot(xc.astype(jnp.bfloat16), _bf16_rows(wg_ref[g]), preferred_element_type=jnp.float32)
        r_gate = jax.nn.sigmoid(z[:, :MXU_TILE] + vec(V_BA, cols))
        i_gate = jax.nn.sigmoid(z[:, MXU_TILE:] + vec(V_BX, cols))
        neg_log_a = r_gate * decay[:, cols]
        a = jnp.exp(-neg_log_a)
        one_minus_a2 = jnp.tanh(neg_log_a) * (a * a + 1.0)
        bx = jnp.sqrt(one_minus_a2) * (i_gate * xc)
        h = h_scr[:, cols]
        hs = []
        for s in range(chunk_t):
            h = a[s * B:(s + 1) * B, :] * h + bx[s * B:(s + 1) * B, :]
            hs.append(h)
        h_scr[:, cols] = h
        y_r.append((jnp.concatenate(hs, axis=0) * sgr[:, cols]).astype(jnp.bfloat16))
    y_r = jnp.concatenate(y_r, axis=-1)

    z_c = jnp.dot(s_act, _bf16_rows(pw2_ref[...]), preferred_element_type=jnp.float32)
    y_c = ((z_c + vec_ref[V_B2:V_B2 + 1, :]) * _silu(in_proj(SEC_GC))).astype(jnp.bfloat16)
    y = jnp.dot(jnp.concatenate([y_r, y_c], axis=-1), _bf16_rows(w_out_ref[...]),
                preferred_element_type=jnp.float32)
    yn = y * jax.lax.rsqrt(jnp.mean(y * y, axis=-1, keepdims=True) + EPS) * vec_ref[V_POST_G:V_POST_G + 1, :]
    y_ref[rows, :] = x + yn


def _layer_kernel(x_ref, h0_ref, rgb0_ref, cvb0_ref, vec_ref, cvw_ref, w_in_ref, wg_ref, pw2_ref, w_out_ref,
                  y_ref, h_out_ref, rgb_out_ref, cvb_out_ref,
                  xr_ext, h_scr, v_ext, *, batch, block_t, chunk_t, n_steps):
    B = batch
    R = chunk_t * B
    blk = block_t * B
    step = pl.program_id(0)

    @pl.when(step == 0)
    def _init():
        xr_ext[:, 0:RG_HIST * B, :] = rgb0_ref[...]
        v_ext[:, 0:CV_HIST * B, :] = cvb0_ref[...]
        h_scr[...] = h0_ref[...]

    decay = _decay(vec_ref)

    def chunk(c, carry):
        _chunk_body(pl.multiple_of(c * R, R), R=R, B=B, decay=decay, x_ref=x_ref, y_ref=y_ref, vec_ref=vec_ref,
                    cvw_ref=cvw_ref, w_in_ref=w_in_ref, wg_ref=wg_ref, pw2_ref=pw2_ref, w_out_ref=w_out_ref,
                    xr_ext=xr_ext, h_scr=h_scr, v_ext=v_ext)
        return carry

    jax.lax.fori_loop(0, block_t // chunk_t, chunk, 0)

    @pl.when(step == n_steps - 1)
    def _emit_state():
        h_out_ref[...] = h_scr[...]
        rgb_out_ref[...] = xr_ext[:, blk:blk + RG_HIST * B, :]
        cvb_out_ref[...] = v_ext[:, blk:blk + CV_HIST * B, :]

    if n_steps > 1:
        @pl.when(step < n_steps - 1)
        def _carry_hist():
            xr_ext[:, 0:RG_HIST * B, :] = xr_ext[:, blk:blk + RG_HIST * B, :]
            v_ext[:, 0:CV_HIST * B, :] = v_ext[:, blk:blk + CV_HIST * B, :]


def _to_slabs(s):
    r, c = s.shape
    return s.reshape(r, c // LANES, LANES).transpose(1, 0, 2)


def _from_slabs(s):
    g, r, l = s.shape
    return s.transpose(1, 0, 2).reshape(r, g * l)


def _stream_layer(x_rows, h0, rgb0, cvb0, weights, *, batch, n_t, block_t, chunk_rows=CHUNK_ROWS):
    B = batch
    assert B % SUBLANES == 0 and chunk_rows % B == 0
    chunk_t = chunk_rows // B
    assert n_t % block_t == 0 and block_t % chunk_t == 0
    n_steps = n_t // block_t
    assert n_steps == 1 or block_t >= CV_HIST
    blk = block_t * B
    f32 = jnp.float32

    def full(a):
        return pl.BlockSpec(a.shape, lambda i: (0,) * a.ndim)

    kern = functools.partial(_layer_kernel, batch=B, block_t=block_t, chunk_t=chunk_t, n_steps=n_steps)
    consts = (h0, rgb0, cvb0) + tuple(weights)
    return pl.pallas_call(
        kern,
        grid=(n_steps,),
        in_specs=[pl.BlockSpec((blk, D_MODEL), lambda i: (i, 0))] + [full(a) for a in consts],
        out_specs=[pl.BlockSpec((blk, D_MODEL), lambda i: (i, 0)),
                   pl.BlockSpec((B, D_RG), lambda i: (0, 0)),
                   pl.BlockSpec((N_LANE_GROUPS, RG_HIST * B, LANES), lambda i: (0, 0, 0)),
                   pl.BlockSpec((N_LANE_GROUPS, CV_HIST * B, LANES), lambda i: (0, 0, 0))],
        out_shape=[jax.ShapeDtypeStruct((n_t * B, D_MODEL), f32),
                   jax.ShapeDtypeStruct((B, D_RG), f32),
                   jax.ShapeDtypeStruct((N_LANE_GROUPS, RG_HIST * B, LANES), f32),
                   jax.ShapeDtypeStruct((N_LANE_GROUPS, CV_HIST * B, LANES), f32)],
        scratch_shapes=[pltpu.VMEM((N_LANE_GROUPS, (RG_HIST + block_t) * B, LANES), f32),
                        pltpu.VMEM((B, D_RG), f32),
                        pltpu.VMEM((N_LANE_GROUPS, (CV_HIST + block_t) * B, LANES), f32)],
        compiler_params=pltpu.CompilerParams(dimension_semantics=("arbitrary",),
                                             vmem_limit_bytes=VMEM_LIMIT_BYTES),
    )(x_rows, *consts)


def _block_diag_gates(wa, wx):
    hpg = MXU_TILE // RG_HEAD_DIM
    eye = jnp.eye(hpg, dtype=wa.dtype)

    def bd(w):
        w4 = w.reshape(N_COL_TILES, hpg, RG_HEAD_DIM, RG_HEAD_DIM)
        return jnp.einsum('ghij,hk->ghikj', w4, eye).reshape(N_COL_TILES, MXU_TILE, MXU_TILE)

    return jnp.concatenate([bd(wa), bd(wx)], axis=-1)


def _to_rows(x):
    b, t, c = x.shape
    return x.transpose(1, 0, 2).reshape(t * b, c)


def _from_rows(x, b):
    tb, c = x.shape
    return x.reshape(tb // b, b, c).transpose(1, 0, 2)


def kernel(x_prompt, x_sample, state_rg_h, state_rg_conv, state_cv_conv, meta_tokens, pre_norm_g, w_in,
           rg_conv_w, rg_conv_b, rg_wa, rg_ba, rg_wx, rg_bx, rg_lambda, cv_conv_w, cv_conv_b, cv_ln_g,
           cv_ln_b, cv_w_pw2, cv_b_pw2, w_out, post_norm_g):
    assert w_in.shape[0] == 1, "single layer"
    f32 = jnp.float32
    bsz, seq, _ = x_prompt.shape
    dbsz, dseq, _ = x_sample.shape

    rows = [pre_norm_g[0], rg_conv_b[0], rg_ba[0], rg_bx[0], rg_lambda[0], cv_ln_g[0], cv_ln_b[0],
            cv_b_pw2[0], post_norm_g[0]]
    vecs = jnp.concatenate([jnp.stack(rows).astype(f32), rg_conv_w[0].astype(f32),
                            jnp.zeros((N_VEC_ROWS - len(rows) - RG_CONV, D_MODEL), f32)], axis=0)
    cvw = _to_slabs(jnp.concatenate([cv_conv_w[0].astype(f32), cv_conv_b[0].astype(f32)[None]], axis=0))
    weights = (vecs, cvw, _pack_rows(w_in[0]), _pack_rows(_block_diag_gates(rg_wa[0], rg_wx[0])),
               _pack_rows(cv_w_pw2[0]), _pack_rows(w_out[0]))

    mb = SUBLANES
    xm = jnp.broadcast_to(meta_tokens.astype(f32)[:, None, :], (N_META, mb, D_MODEL)).reshape(N_META * mb, D_MODEL)
    _, h_m, rgb_m, cvb_m = _stream_layer(
        xm, jnp.zeros((mb, D_RG), f32), jnp.zeros((N_LANE_GROUPS, RG_HIST * mb, LANES), f32),
        jnp.zeros((N_LANE_GROUPS, CV_HIST * mb, LANES), f32), weights, batch=mb, n_t=N_META, block_t=N_META)

    def rebatch(s, n_hist):
        lead = s.shape[:-2]
        s = s.reshape(*lead, n_hist, mb, s.shape[-1])[..., :1, :]
        return jnp.broadcast_to(s, (*lead, n_hist, bsz, s.shape[-1])).reshape(*lead, n_hist * bsz, s.shape[-1])

    yp, h_p, rgb_p, cvb_p = _stream_layer(
        _to_rows(x_prompt), rebatch(h_m, 1), rebatch(rgb_m, RG_HIST), rebatch(cvb_m, CV_HIST),
        weights, batch=bsz, n_t=seq, block_t=32)

    ys, h_s, rgb_s, cvb_s = _stream_layer(
        _to_rows(x_sample), state_rg_h[0], _to_slabs(_to_rows(state_rg_conv[0])),
        _to_slabs(_to_rows(state_cv_conv[0])), weights, batch=dbsz, n_t=dseq, block_t=dseq)

    def conv_state(s, b):
        return _from_rows(_from_slabs(s), b)[None]

    return (_from_rows(yp, bsz), _from_rows(ys, dbsz),
            h_p[None], conv_state(rgb_p, bsz), conv_state(cvb_p, bsz),
            h_s[None], conv_state(rgb_s, dbsz), conv_state(cvb_s, dbsz))
```

```python
import functools

import jax
import jax.numpy as jnp
from jax.experimental import pallas as pl
from jax.experimental.pallas import tpu as pltpu

D_MODEL = 1024
N_META = 16
D_RG = 1024
N_RG_HEADS = 16
RG_HEAD_DIM = D_RG // N_RG_HEADS
RG_CONV = 4
RG_C = 8.0
D_CV = 1024
CV_CONV = 31
EPS = 1e-6

RG_HIST = RG_CONV - 1
CV_HIST = CV_CONV - 1
SUBLANES, LANES = 8, 128
PACKED_ROWS = 16
PACK_BLOCK_ROWS = 128
MXU_TILE = 256
N_COL_TILES = D_RG // MXU_TILE
LANES_PER_TILE = MXU_TILE // LANES
N_LANE_GROUPS = D_CV // LANES
N_IN_SECTIONS = 5
SEC_XR, SEC_GR, SEC_CVV, SEC_CVG, SEC_GC = range(N_IN_SECTIONS)
CHUNK_ROWS = 128
VMEM_LIMIT_BYTES = 56 * 1024 * 1024

V_PRE_G, V_RG_CB, V_BA, V_BX, V_LAM, V_LN_G, V_LN_B, V_B2, V_POST_G, V_RG_CW = range(10)
N_VEC_ROWS = 16
CVW_ROWS = 32


def _silu(x):
    return x * jax.nn.sigmoid(x)


def _bf16_rows(w_u32):
    return pltpu.bitcast(w_u32, jnp.bfloat16)


def _pack_rows_kernel(w_ref, o_ref):
    packed = pltpu.bitcast(w_ref[...].astype(jnp.bfloat16), jnp.uint32)
    if len(o_ref.shape) == 2:
        o_ref[...] = packed
    else:
        tile_cols = o_ref.shape[-1]
        for t in range(o_ref.shape[0]):
            o_ref[t] = packed[:, t * tile_cols:(t + 1) * tile_cols]


def _pack_rows(w, tile_cols=None):
    k, n = w.shape
    rows = min(k, PACK_BLOCK_ROWS)
    assert k % rows == 0 and rows % (2 * PACKED_ROWS) == 0
    if tile_cols is None:
        out_spec = pl.BlockSpec((rows // 2, n), lambda i: (i, 0))
        out_shape = jax.ShapeDtypeStruct((k // 2, n), jnp.uint32)
    else:
        assert n % tile_cols == 0
        out_spec = pl.BlockSpec((n // tile_cols, rows // 2, tile_cols), lambda i: (0, i, 0))
        out_shape = jax.ShapeDtypeStruct((n // tile_cols, k // 2, tile_cols), jnp.uint32)
    return pl.pallas_call(
        _pack_rows_kernel,
        grid=(k // rows,),
        in_specs=[pl.BlockSpec((rows, n), lambda i: (i, 0))],
        out_specs=out_spec,
        out_shape=out_shape,
    )(w)


def _decay(vec_ref):
    lam = vec_ref[V_LAM:V_LAM + 1, :]
    return RG_C * (jnp.maximum(-lam, 0.0) + jnp.log1p(jnp.exp(-jnp.abs(lam))))


def _row_sum(parts):
    s = parts[0]
    for p in parts[1:]:
        s = s + p
    return jnp.sum(s, axis=-1, keepdims=True)


def _chunk_body(r0, *, R, B, decay, x_ref, y_ref, vec_ref, cvw_ref, w_in_ref, wg_ref, pw2_ref, w_out_ref,
                xn_scr, cv_scr, xr_ext, h_scr, v_ext):
    chunk_t = R // B
    rows = pl.ds(r0, R)
    n_tiles = R // SUBLANES
    step = B // SUBLANES

    def vec(row, cols):
        return vec_ref[row:row + 1, cols]

    x = x_ref[rows, :]
    xn = x * jax.lax.rsqrt(jnp.mean(x * x, axis=-1, keepdims=True) + EPS) * vec_ref[V_PRE_G:V_PRE_G + 1, :]
    xn_scr[...] = xn.astype(jnp.bfloat16)

    def in_proj_tile(sec, g):
        return jnp.dot(xn_scr[...], _bf16_rows(w_in_ref[sec * N_COL_TILES + g]), preferred_element_type=jnp.float32)

    for g in range(N_COL_TILES):
        v = in_proj_tile(SEC_CVV, g) * jax.nn.sigmoid(in_proj_tile(SEC_CVG, g))
        for half in range(LANES_PER_TILE):
            v_ext[g * LANES_PER_TILE + half, pl.ds(pl.multiple_of(CV_HIST * B + r0, 8), R), :] = (
                v[:, half * LANES:(half + 1) * LANES])

    def conv_lane_group(j, carry):
        acc = [jnp.broadcast_to(cvw_ref[j, CV_CONV:CV_CONV + 1, :], (SUBLANES, LANES))] * n_tiles
        for k in range(CV_CONV):
            w = jnp.broadcast_to(cvw_ref[j, k:k + 1, :], (SUBLANES, LANES))
            for i in range(n_tiles):
                row = pl.multiple_of(r0 + (i + step * k) * SUBLANES, SUBLANES)
                acc[i] = acc[i] + w * v_ext[j, pl.ds(row, SUBLANES), :]
        cv_scr[j] = jnp.concatenate(acc, axis=0)
        return carry

    jax.lax.fori_loop(0, N_LANE_GROUPS, conv_lane_group, 0)

    cv = [cv_scr[j] for j in range(N_LANE_GROUPS)]
    inv_d = 1.0 / D_CV
    mu = _row_sum(cv) * inv_d
    cvc = [p - mu for p in cv]
    var = _row_sum([p * p for p in cvc]) * inv_d
    rs = jax.lax.rsqrt(var + EPS)
    s_act = []
    for j in range(N_LANE_GROUPS):
        lcols = slice(j * LANES, (j + 1) * LANES)
        ln = cvc[j] * rs * vec(V_LN_G, lcols) + vec(V_LN_B, lcols)
        s_act.append(_silu(ln).astype(jnp.bfloat16))
    s_act = jnp.concatenate(s_act, axis=-1)

    for g in range(N_COL_TILES):
        u = in_proj_tile(SEC_XR, g)
        for half in range(LANES_PER_TILE):
            xr_ext[g * LANES_PER_TILE + half, pl.ds(pl.multiple_of(RG_HIST * B + r0, 8), R), :] = (
                u[:, half * LANES:(half + 1) * LANES])
    y_r = []
    for g in range(N_COL_TILES):
        cols = slice(g * MXU_TILE, (g + 1) * MXU_TILE)
        sgr = _silu(in_proj_tile(SEC_GR, g))
        halves = []
        for half in range(LANES_PER_TILE):
            j = g * LANES_PER_TILE + half
            lcols = slice(j * LANES, (j + 1) * LANES)
            acc = vec(V_RG_CB, lcols)
            for k in range(RG_CONV):
                tap = xr_ext[j, pl.ds(pl.multiple_of(k * B + r0, 8), R), :]
                acc = acc + vec(V_RG_CW + k, lcols) * tap
            halves.append(acc)
        xc = jnp.concatenate(halves, axis=-1)
        z = jnp.dot(xc.astype(jnp.bfloat16), _bf16_rows(wg_ref[g]), preferred_element_type=jnp.float32)
        r_gate = jax.nn.sigmoid(z[:, :MXU_TILE] + vec(V_BA, cols))
        i_gate = jax.nn.sigmoid(z[:, MXU_TILE:] + vec(V_BX, cols))
        neg_log_a = r_gate * decay[:, cols]
        a = jnp.exp(-neg_log_a)
        one_minus_a2 = jnp.tanh(neg_log_a) * (a * a + 1.0)
        bx = jnp.sqrt(one_minus_a2) * (i_gate * xc)
        h = h_scr[:, cols]
        hs = []
        for s in range(chunk_t):
            h = a[s * B:(s + 1) * B, :] * h + bx[s * B:(s + 1) * B, :]
            hs.append(h)
        h_scr[:, cols] = h
        y_r.append((jnp.concatenate(hs, axis=0) * sgr).astype(jnp.bfloat16))
    y_r = jnp.concatenate(y_r, axis=-1)

    z_c = jnp.dot(s_act, _bf16_rows(pw2_ref[...]), preferred_element_type=jnp.float32)
    sgc = jnp.concatenate([_silu(in_proj_tile(SEC_GC, g)) for g in range(N_COL_TILES)], axis=-1)
    y_c = ((z_c + vec_ref[V_B2:V_B2 + 1, :]) * sgc).astype(jnp.bfloat16)
    y = jnp.dot(jnp.concatenate([y_r, y_c], axis=-1), _bf16_rows(w_out_ref[...]),
                preferred_element_type=jnp.float32)
    yn = y * jax.lax.rsqrt(jnp.mean(y * y, axis=-1, keepdims=True) + EPS) * vec_ref[V_POST_G:V_POST_G + 1, :]
    y_ref[rows, :] = x_ref[rows, :] + yn


def _layer_kernel(x_ref, h0_ref, rgb0_ref, cvb0_ref, vec_ref, cvw_ref, w_in_ref, wg_ref, pw2_ref, w_out_ref,
                  y_ref, h_out_ref, rgb_out_ref, cvb_out_ref,
                  xn_scr, cv_scr, xr_ext, h_scr, v_ext, *, batch, block_t, chunk_t, n_steps):
    B = batch
    R = chunk_t * B
    blk = block_t * B
    step = pl.program_id(0)

    @pl.when(step == 0)
    def _init():
        xr_ext[:, 0:RG_HIST * B, :] = rgb0_ref[...]
        v_ext[:, 0:CV_HIST * B, :] = cvb0_ref[...]
        h_scr[...] = h0_ref[...]

    decay = _decay(vec_ref)

    def chunk(c, carry):
        _chunk_body(pl.multiple_of(c * R, R), R=R, B=B, decay=decay, x_ref=x_ref, y_ref=y_ref, vec_ref=vec_ref,
                    cvw_ref=cvw_ref, w_in_ref=w_in_ref, wg_ref=wg_ref, pw2_ref=pw2_ref, w_out_ref=w_out_ref,
                    xn_scr=xn_scr, cv_scr=cv_scr, xr_ext=xr_ext, h_scr=h_scr, v_ext=v_ext)
        return carry

    jax.lax.fori_loop(0, block_t // chunk_t, chunk, 0)

    @pl.when(step == n_steps - 1)
    def _emit_state():
        h_out_ref[...] = h_scr[...]
        rgb_out_ref[...] = xr_ext[:, blk:blk + RG_HIST * B, :]
        cvb_out_ref[...] = v_ext[:, blk:blk + CV_HIST * B, :]

    if n_steps > 1:
        @pl.when(step < n_steps - 1)
        def _carry_hist():
            xr_ext[:, 0:RG_HIST * B, :] = xr_ext[:, blk:blk + RG_HIST * B, :]
            v_ext[:, 0:CV_HIST * B, :] = v_ext[:, blk:blk + CV_HIST * B, :]


def _to_slabs(s):
    r, c = s.shape
    return s.reshape(r, c // LANES, LANES).transpose(1, 0, 2)


def _from_slabs(s):
    g, r, l = s.shape
    return s.transpose(1, 0, 2).reshape(r, g * l)


def _stream_layer(x_rows, h0, rgb0, cvb0, weights, *, batch, n_t, block_t, chunk_rows=CHUNK_ROWS):
    B = batch
    assert B % SUBLANES == 0 and chunk_rows % B == 0
    chunk_t = chunk_rows // B
    assert n_t % block_t == 0 and block_t % chunk_t == 0
    n_steps = n_t // block_t
    assert n_steps == 1 or block_t >= CV_HIST
    blk = block_t * B
    f32 = jnp.float32

    def full(a):
        return pl.BlockSpec(a.shape, lambda i: (0,) * a.ndim)

    kern = functools.partial(_layer_kernel, batch=B, block_t=block_t, chunk_t=chunk_t, n_steps=n_steps)
    consts = (h0, rgb0, cvb0) + tuple(weights)
    return pl.pallas_call(
        kern,
        grid=(n_steps,),
        in_specs=[pl.BlockSpec((blk, D_MODEL), lambda i: (i, 0))] + [full(a) for a in consts],
        out_specs=[pl.BlockSpec((blk, D_MODEL), lambda i: (i, 0)),
                   pl.BlockSpec((B, D_RG), lambda i: (0, 0)),
                   pl.BlockSpec((N_LANE_GROUPS, RG_HIST * B, LANES), lambda i: (0, 0, 0)),
                   pl.BlockSpec((N_LANE_GROUPS, CV_HIST * B, LANES), lambda i: (0, 0, 0))],
        out_shape=[jax.ShapeDtypeStruct((n_t * B, D_MODEL), f32),
                   jax.ShapeDtypeStruct((B, D_RG), f32),
                   jax.ShapeDtypeStruct((N_LANE_GROUPS, RG_HIST * B, LANES), f32),
                   jax.ShapeDtypeStruct((N_LANE_GROUPS, CV_HIST * B, LANES), f32)],
        scratch_shapes=[pltpu.VMEM((chunk_rows, D_MODEL), jnp.bfloat16),
                        pltpu.VMEM((N_LANE_GROUPS, chunk_rows, LANES), f32),
                        pltpu.VMEM((N_LANE_GROUPS, (RG_HIST + block_t) * B, LANES), f32),
                        pltpu.VMEM((B, D_RG), f32),
                        pltpu.VMEM((N_LANE_GROUPS, (CV_HIST + block_t) * B, LANES), f32)],
        compiler_params=pltpu.CompilerParams(dimension_semantics=("arbitrary",),
                                             vmem_limit_bytes=VMEM_LIMIT_BYTES),
    )(x_rows, *consts)


def _block_diag_gates(wa, wx):
    hpg = MXU_TILE // RG_HEAD_DIM
    eye = jnp.eye(hpg, dtype=wa.dtype)

    def bd(w):
        w4 = w.reshape(N_COL_TILES, hpg, RG_HEAD_DIM, RG_HEAD_DIM)
        return jnp.einsum('ghij,hk->ghikj', w4, eye).reshape(N_COL_TILES, MXU_TILE, MXU_TILE)

    return jnp.concatenate([bd(wa), bd(wx)], axis=-1)


def _to_rows(x):
    b, t, c = x.shape
    return x.transpose(1, 0, 2).reshape(t * b, c)


def _from_rows(x, b):
    tb, c = x.shape
    return x.reshape(tb // b, b, c).transpose(1, 0, 2)


def kernel(x_prompt, x_sample, state_rg_h, state_rg_conv, state_cv_conv, meta_tokens, pre_norm_g, w_in,
           rg_conv_w, rg_conv_b, rg_wa, rg_ba, rg_wx, rg_bx, rg_lambda, cv_conv_w, cv_conv_b, cv_ln_g,
           cv_ln_b, cv_w_pw2, cv_b_pw2, w_out, post_norm_g):
    assert w_in.shape[0] == 1, "single layer"
    f32 = jnp.float32
    bsz, seq, _ = x_prompt.shape
    dbsz, dseq, _ = x_sample.shape

    rows = [pre_norm_g[0], rg_conv_b[0], rg_ba[0], rg_bx[0], rg_lambda[0], cv_ln_g[0], cv_ln_b[0],
            cv_b_pw2[0], post_norm_g[0]]
    vecs = jnp.concatenate([jnp.stack(rows).astype(f32), rg_conv_w[0].astype(f32),
                            jnp.zeros((N_VEC_ROWS - len(rows) - RG_CONV, D_MODEL), f32)], axis=0)
    cvw = _to_slabs(jnp.concatenate([cv_conv_w[0].astype(f32), cv_conv_b[0].astype(f32)[None]], axis=0))
    wg = _block_diag_gates(rg_wa[0], rg_wx[0])
    wg = _pack_rows(wg.reshape(N_COL_TILES * MXU_TILE, 2 * MXU_TILE)).reshape(N_COL_TILES, MXU_TILE // 2, 2 * MXU_TILE)
    w_in_tiles = _pack_rows(w_in[0], tile_cols=MXU_TILE)
    weights = (vecs, cvw, w_in_tiles, wg, _pack_rows(cv_w_pw2[0]), _pack_rows(w_out[0]))

    mb = SUBLANES
    xm = jnp.broadcast_to(meta_tokens.astype(f32)[:, None, :], (N_META, mb, D_MODEL)).reshape(N_META * mb, D_MODEL)
    _, h_m, rgb_m, cvb_m = _stream_layer(
        xm, jnp.zeros((mb, D_RG), f32), jnp.zeros((N_LANE_GROUPS, RG_HIST * mb, LANES), f32),
        jnp.zeros((N_LANE_GROUPS, CV_HIST * mb, LANES), f32), weights, batch=mb, n_t=N_META, block_t=N_META)

    def rebatch(s, n_hist):
        lead = s.shape[:-2]
        s = s.reshape(*lead, n_hist, mb, s.shape[-1])[..., :1, :]
        return jnp.broadcast_to(s, (*lead, n_hist, bsz, s.shape[-1])).reshape(*lead, n_hist * bsz, s.shape[-1])

    yp, h_p, rgb_p, cvb_p = _stream_layer(
        _to_rows(x_prompt), rebatch(h_m, 1), rebatch(rgb_m, RG_HIST), rebatch(cvb_m, CV_HIST),
        weights, batch=bsz, n_t=seq, block_t=32, chunk_rows=256)

    ys, h_s, rgb_s, cvb_s = _stream_layer(
        _to_rows(x_sample), state_rg_h[0], _to_slabs(_to_rows(state_rg_conv[0])),
        _to_slabs(_to_rows(state_cv_conv[0])), weights, batch=dbsz, n_t=dseq, block_t=dseq, chunk_rows=256)

    def conv_state(s, b):
        return _from_rows(_from_slabs(s), b)[None]

    return (_from_rows(yp, bsz), _from_rows(ys, dbsz),
            h_p[None], conv_state(rgb_p, bsz), conv_state(cvb_p, bsz),
            h_s[None], conv_state(rgb_s, dbsz), conv_state(cvb_s, dbsz))
```

```python
import functools

import jax
import jax.numpy as jnp
from jax.experimental import pallas as pl
from jax.experimental.pallas import tpu as pltpu

D_MODEL = 1024
N_META = 16
D_RG = 1024
N_RG_HEADS = 16
RG_HEAD_DIM = D_RG // N_RG_HEADS
RG_CONV = 4
RG_C = 8.0
D_CV = 1024
CV_CONV = 31
EPS = 1e-6

RG_HIST = RG_CONV - 1
CV_HIST = CV_CONV - 1
SUBLANES, LANES = 8, 128
PACKED_ROWS = 16
PACK_BLOCK_ROWS = 128
MXU_TILE = 256
N_COL_TILES = D_RG // MXU_TILE
LANES_PER_TILE = MXU_TILE // LANES
N_LANE_GROUPS = D_CV // LANES
N_IN_SECTIONS = 5
SEC_XR, SEC_GR, SEC_CVV, SEC_CVG, SEC_GC = range(N_IN_SECTIONS)
CHUNK_ROWS = 128
CONV_TAP_GROUP = 4
VMEM_LIMIT_BYTES = 56 * 1024 * 1024

V_PRE_G, V_RG_CB, V_BA, V_BX, V_LAM, V_LN_G, V_LN_B, V_B2, V_POST_G, V_RG_CW = range(10)
N_VEC_ROWS = 16
CVW_ROWS = 32


def _silu(x):
    return x * jax.nn.sigmoid(x)


def _bf16_rows(w_u32):
    return pltpu.bitcast(w_u32, jnp.bfloat16)


def _pack_rows_kernel(w_ref, o_ref):
    packed = pltpu.bitcast(w_ref[...].astype(jnp.bfloat16), jnp.uint32)
    if len(o_ref.shape) == 2:
        o_ref[...] = packed
    else:
        tile_cols = o_ref.shape[-1]
        for t in range(o_ref.shape[0]):
            o_ref[t] = packed[:, t * tile_cols:(t + 1) * tile_cols]


def _pack_rows(w, tile_cols=None):
    k, n = w.shape
    rows = min(k, PACK_BLOCK_ROWS)
    assert k % rows == 0 and rows % (2 * PACKED_ROWS) == 0
    if tile_cols is None:
        out_spec = pl.BlockSpec((rows // 2, n), lambda i: (i, 0))
        out_shape = jax.ShapeDtypeStruct((k // 2, n), jnp.uint32)
    else:
        assert n % tile_cols == 0
        out_spec = pl.BlockSpec((n // tile_cols, rows // 2, tile_cols), lambda i: (0, i, 0))
        out_shape = jax.ShapeDtypeStruct((n // tile_cols, k // 2, tile_cols), jnp.uint32)
    return pl.pallas_call(
        _pack_rows_kernel,
        grid=(k // rows,),
        in_specs=[pl.BlockSpec((rows, n), lambda i: (i, 0))],
        out_specs=out_spec,
        out_shape=out_shape,
    )(w)


def _decay(vec_ref):
    lam = vec_ref[V_LAM:V_LAM + 1, :]
    return RG_C * (jnp.maximum(-lam, 0.0) + jnp.log1p(jnp.exp(-jnp.abs(lam))))


def _row_sum(parts):
    s = parts[0]
    for p in parts[1:]:
        s = s + p
    return jnp.sum(s, axis=-1, keepdims=True)


def _chunk_body(r0, *, R, B, decay, x_ref, y_ref, vec_ref, cvw_ref, w_in_ref, wg_ref, pw2_ref, w_out_ref,
                xn_scr, cv_scr, xr_ext, h_scr, v_ext, vb_ext):
    chunk_t = R // B
    rows = pl.ds(r0, R)
    n_ptiles = R // PACKED_ROWS
    pstep = B // PACKED_ROWS

    def vec(row, cols):
        return vec_ref[row:row + 1, cols]

    x = x_ref[rows, :]
    xn = x * jax.lax.rsqrt(jnp.mean(x * x, axis=-1, keepdims=True) + EPS) * vec_ref[V_PRE_G:V_PRE_G + 1, :]
    xn_scr[...] = xn.astype(jnp.bfloat16)

    def in_proj_tile(sec, g):
        return jnp.dot(xn_scr[...], _bf16_rows(w_in_ref[sec * N_COL_TILES + g]), preferred_element_type=jnp.float32)

    for g in range(N_COL_TILES):
        v = in_proj_tile(SEC_CVV, g) * jax.nn.sigmoid(in_proj_tile(SEC_CVG, g))
        vb = v.astype(jnp.bfloat16)
        for half in range(LANES_PER_TILE):
            j = g * LANES_PER_TILE + half
            lcols = slice(half * LANES, (half + 1) * LANES)
            v_ext[j, pl.ds(pl.multiple_of(CV_HIST * B + r0, SUBLANES), R), :] = v[:, lcols]
            vb_ext[j, pl.ds(pl.multiple_of(CV_HIST * B + r0, PACKED_ROWS), R), :] = vb[:, lcols]

    def conv_lane_group(j, carry):
        acc = [jnp.broadcast_to(cvw_ref[j, CV_CONV:CV_CONV + 1, :], (PACKED_ROWS, LANES))] * n_ptiles
        for k0 in range(0, CV_CONV, CONV_TAP_GROUP):
            taps = range(k0, min(k0 + CONV_TAP_GROUP, CV_CONV))
            w = {k: jnp.broadcast_to(cvw_ref[j, k:k + 1, :], (PACKED_ROWS, LANES)).astype(jnp.bfloat16)
                 for k in taps}
            for i in range(n_ptiles):
                part = None
                for k in taps:
                    row = pl.multiple_of(r0 + (i + pstep * k) * PACKED_ROWS, PACKED_ROWS)
                    prod = w[k] * vb_ext[j, pl.ds(row, PACKED_ROWS), :]
                    part = prod if part is None else part + prod
                acc[i] = acc[i] + part.astype(jnp.float32)
        cv_scr[j] = jnp.concatenate(acc, axis=0)
        return carry

    jax.lax.fori_loop(0, N_LANE_GROUPS, conv_lane_group, 0)

    cv = [cv_scr[j] for j in range(N_LANE_GROUPS)]
    inv_d = 1.0 / D_CV
    mu = _row_sum(cv) * inv_d
    cvc = [p - mu for p in cv]
    var = _row_sum([p * p for p in cvc]) * inv_d
    rs = jax.lax.rsqrt(var + EPS)
    s_act = []
    for j in range(N_LANE_GROUPS):
        lcols = slice(j * LANES, (j + 1) * LANES)
        ln = cvc[j] * rs * vec(V_LN_G, lcols) + vec(V_LN_B, lcols)
        s_act.append(_silu(ln).astype(jnp.bfloat16))
    s_act = jnp.concatenate(s_act, axis=-1)

    for g in range(N_COL_TILES):
        u = in_proj_tile(SEC_XR, g)
        for half in range(LANES_PER_TILE):
            xr_ext[g * LANES_PER_TILE + half, pl.ds(pl.multiple_of(RG_HIST * B + r0, 8), R), :] = (
                u[:, half * LANES:(half + 1) * LANES])
    y_r = []
    for g in range(N_COL_TILES):
        cols = slice(g * MXU_TILE, (g + 1) * MXU_TILE)
        sgr = _silu(in_proj_tile(SEC_GR, g))
        halves = []
        for half in range(LANES_PER_TILE):
            j = g * LANES_PER_TILE + half
            lcols = slice(j * LANES, (j + 1) * LANES)
            acc = vec(V_RG_CB, lcols)
            for k in range(RG_CONV):
                tap = xr_ext[j, pl.ds(pl.multiple_of(k * B + r0, 8), R), :]
                acc = acc + vec(V_RG_CW + k, lcols) * tap
            halves.append(acc)
        xc = jnp.concatenate(halves, axis=-1)
        z = jnp.dot(xc.astype(jnp.bfloat16), _bf16_rows(wg_ref[g]), preferred_element_type=jnp.float32)
        r_gate = jax.nn.sigmoid(z[:, :MXU_TILE] + vec(V_BA, cols))
        i_gate = jax.nn.sigmoid(z[:, MXU_TILE:] + vec(V_BX, cols))
        neg_log_a = r_gate * decay[:, cols]
        a = jnp.exp(-neg_log_a)
        one_minus_a2 = jnp.tanh(neg_log_a) * (a * a + 1.0)
        bx = jnp.sqrt(one_minus_a2) * (i_gate * xc)
        h = h_scr[:, cols]
        hs = []
        for s in range(chunk_t):
            h = a[s * B:(s + 1) * B, :] * h + bx[s * B:(s + 1) * B, :]
            hs.append(h)
        h_scr[:, cols] = h
        y_r.append((jnp.concatenate(hs, axis=0) * sgr).astype(jnp.bfloat16))
    y_r = jnp.concatenate(y_r, axis=-1)

    z_c = jnp.dot(s_act, _bf16_rows(pw2_ref[...]), preferred_element_type=jnp.float32)
    sgc = jnp.concatenate([_silu(in_proj_tile(SEC_GC, g)) for g in range(N_COL_TILES)], axis=-1)
    y_c = ((z_c + vec_ref[V_B2:V_B2 + 1, :]) * sgc).astype(jnp.bfloat16)
    y = jnp.dot(jnp.concatenate([y_r, y_c], axis=-1), _bf16_rows(w_out_ref[...]),
                preferred_element_type=jnp.float32)
    yn = y * jax.lax.rsqrt(jnp.mean(y * y, axis=-1, keepdims=True) + EPS) * vec_ref[V_POST_G:V_POST_G + 1, :]
    y_ref[rows, :] = x_ref[rows, :] + yn


def _layer_kernel(x_ref, h0_ref, rgb0_ref, cvb0_ref, vec_ref, cvw_ref, w_in_ref, wg_ref, pw2_ref, w_out_ref,
                  y_ref, h_out_ref, rgb_out_ref, cvb_out_ref,
                  xn_scr, cv_scr, xr_ext, h_scr, v_ext, vb_ext, *, batch, block_t, chunk_t, n_steps):
    B = batch
    R = chunk_t * B
    blk = block_t * B
    step = pl.program_id(0)

    @pl.when(step == 0)
    def _init():
        xr_ext[:, 0:RG_HIST * B, :] = rgb0_ref[...]
        v_ext[:, 0:CV_HIST * B, :] = cvb0_ref[...]
        vb_ext[:, 0:CV_HIST * B, :] = cvb0_ref[...].astype(vb_ext.dtype)
        h_scr[...] = h0_ref[...]

    decay = _decay(vec_ref)

    def chunk(c, carry):
        _chunk_body(pl.multiple_of(c * R, R), R=R, B=B, decay=decay, x_ref=x_ref, y_ref=y_ref, vec_ref=vec_ref,
                    cvw_ref=cvw_ref, w_in_ref=w_in_ref, wg_ref=wg_ref, pw2_ref=pw2_ref, w_out_ref=w_out_ref,
                    xn_scr=xn_scr, cv_scr=cv_scr, xr_ext=xr_ext, h_scr=h_scr, v_ext=v_ext, vb_ext=vb_ext)
        return carry

    jax.lax.fori_loop(0, block_t // chunk_t, chunk, 0)

    @pl.when(step == n_steps - 1)
    def _emit_state():
        h_out_ref[...] = h_scr[...]
        rgb_out_ref[...] = xr_ext[:, blk:blk + RG_HIST * B, :]
        cvb_out_ref[...] = v_ext[:, blk:blk + CV_HIST * B, :]

    if n_steps > 1:
        @pl.when(step < n_steps - 1)
        def _carry_hist():
            xr_ext[:, 0:RG_HIST * B, :] = xr_ext[:, blk:blk + RG_HIST * B, :]
            v_ext[:, 0:CV_HIST * B, :] = v_ext[:, blk:blk + CV_HIST * B, :]
            vb_ext[:, 0:CV_HIST * B, :] = vb_ext[:, blk:blk + CV_HIST * B, :]


def _to_slabs(s):
    r, c = s.shape
    return s.reshape(r, c // LANES, LANES).transpose(1, 0, 2)


def _from_slabs(s):
    g, r, l = s.shape
    return s.transpose(1, 0, 2).reshape(r, g * l)


def _stream_layer(x_rows, h0, rgb0, cvb0, weights, *, batch, n_t, block_t, chunk_rows=CHUNK_ROWS):
    B = batch
    assert B % PACKED_ROWS == 0 and chunk_rows % B == 0
    chunk_t = chunk_rows // B
    assert n_t % block_t == 0 and block_t % chunk_t == 0
    n_steps = n_t // block_t
    assert n_steps == 1 or block_t >= CV_HIST
    blk = block_t * B
    f32 = jnp.float32

    def full(a):
        return pl.BlockSpec(a.shape, lambda i: (0,) * a.ndim)

    kern = functools.partial(_layer_kernel, batch=B, block_t=block_t, chunk_t=chunk_t, n_steps=n_steps)
    consts = (h0, rgb0, cvb0) + tuple(weights)
    return pl.pallas_call(
        kern,
        grid=(n_steps,),
        in_specs=[pl.BlockSpec((blk, D_MODEL), lambda i: (i, 0))] + [full(a) for a in consts],
        out_specs=[pl.BlockSpec((blk, D_MODEL), lambda i: (i, 0)),
                   pl.BlockSpec((B, D_RG), lambda i: (0, 0)),
                   pl.BlockSpec((N_LANE_GROUPS, RG_HIST * B, LANES), lambda i: (0, 0, 0)),
                   pl.BlockSpec((N_LANE_GROUPS, CV_HIST * B, LANES), lambda i: (0, 0, 0))],
        out_shape=[jax.ShapeDtypeStruct((n_t * B, D_MODEL), f32),
                   jax.ShapeDtypeStruct((B, D_RG), f32),
                   jax.ShapeDtypeStruct((N_LANE_GROUPS, RG_HIST * B, LANES), f32),
                   jax.ShapeDtypeStruct((N_LANE_GROUPS, CV_HIST * B, LANES), f32)],
        scratch_shapes=[pltpu.VMEM((chunk_rows, D_MODEL), jnp.bfloat16),
                        pltpu.VMEM((N_LANE_GROUPS, chunk_rows, LANES), f32),
                        pltpu.VMEM((N_LANE_GROUPS, (RG_HIST + block_t) * B, LANES), f32),
                        pltpu.VMEM((B, D_RG), f32),
                        pltpu.VMEM((N_LANE_GROUPS, (CV_HIST + block_t) * B, LANES), f32),
                        pltpu.VMEM((N_LANE_GROUPS, (CV_HIST + block_t) * B, LANES), jnp.bfloat16)],
        compiler_params=pltpu.CompilerParams(dimension_semantics=("arbitrary",),
                                             vmem_limit_bytes=VMEM_LIMIT_BYTES),
    )(x_rows, *consts)


def _block_diag_gates(wa, wx):
    hpg = MXU_TILE // RG_HEAD_DIM
    eye = jnp.eye(hpg, dtype=wa.dtype)

    def bd(w):
        w4 = w.reshape(N_COL_TILES, hpg, RG_HEAD_DIM, RG_HEAD_DIM)
        return jnp.einsum('ghij,hk->ghikj', w4, eye).reshape(N_COL_TILES, MXU_TILE, MXU_TILE)

    return jnp.concatenate([bd(wa), bd(wx)], axis=-1)


def _to_rows(x):
    b, t, c = x.shape
    return x.transpose(1, 0, 2).reshape(t * b, c)


def _from_rows(x, b):
    tb, c = x.shape
    return x.reshape(tb // b, b, c).transpose(1, 0, 2)


def kernel(x_prompt, x_sample, state_rg_h, state_rg_conv, state_cv_conv, meta_tokens, pre_norm_g, w_in,
           rg_conv_w, rg_conv_b, rg_wa, rg_ba, rg_wx, rg_bx, rg_lambda, cv_conv_w, cv_conv_b, cv_ln_g,
           cv_ln_b, cv_w_pw2, cv_b_pw2, w_out, post_norm_g):
    assert w_in.shape[0] == 1, "single layer"
    f32 = jnp.float32
    bsz, seq, _ = x_prompt.shape
    dbsz, dseq, _ = x_sample.shape

    rows = [pre_norm_g[0], rg_conv_b[0], rg_ba[0], rg_bx[0], rg_lambda[0], cv_ln_g[0], cv_ln_b[0],
            cv_b_pw2[0], post_norm_g[0]]
    vecs = jnp.concatenate([jnp.stack(rows).astype(f32), rg_conv_w[0].astype(f32),
                            jnp.zeros((N_VEC_ROWS - len(rows) - RG_CONV, D_MODEL), f32)], axis=0)
    cvw = _to_slabs(jnp.concatenate([cv_conv_w[0].astype(f32), cv_conv_b[0].astype(f32)[None]], axis=0))
    wg = _block_diag_gates(rg_wa[0], rg_wx[0])
    wg = _pack_rows(wg.reshape(N_COL_TILES * MXU_TILE, 2 * MXU_TILE)).reshape(N_COL_TILES, MXU_TILE // 2, 2 * MXU_TILE)
    w_in_tiles = _pack_rows(w_in[0], tile_cols=MXU_TILE)
    weights = (vecs, cvw, w_in_tiles, wg, _pack_rows(cv_w_pw2[0]), _pack_rows(w_out[0]))

    mb = PACKED_ROWS
    xm = jnp.broadcast_to(meta_tokens.astype(f32)[:, None, :], (N_META, mb, D_MODEL)).reshape(N_META * mb, D_MODEL)
    _, h_m, rgb_m, cvb_m = _stream_layer(
        xm, jnp.zeros((mb, D_RG), f32), jnp.zeros((N_LANE_GROUPS, RG_HIST * mb, LANES), f32),
        jnp.zeros((N_LANE_GROUPS, CV_HIST * mb, LANES), f32), weights, batch=mb, n_t=N_META, block_t=N_META)

    def rebatch(s, n_hist):
        lead = s.shape[:-2]
        s = s.reshape(*lead, n_hist, mb, s.shape[-1])[..., :1, :]
        return jnp.broadcast_to(s, (*lead, n_hist, bsz, s.shape[-1])).reshape(*lead, n_hist * bsz, s.shape[-1])

    yp, h_p, rgb_p, cvb_p = _stream_layer(
        _to_rows(x_prompt), rebatch(h_m, 1), rebatch(rgb_m, RG_HIST), rebatch(cvb_m, CV_HIST),
        weights, batch=bsz, n_t=seq, block_t=32, chunk_rows=256)

    ys, h_s, rgb_s, cvb_s = _stream_layer(
        _to_rows(x_sample), state_rg_h[0], _to_slabs(_to_rows(state_rg_conv[0])),
        _to_slabs(_to_rows(state_cv_conv[0])), weights, batch=dbsz, n_t=dseq, block_t=dseq, chunk_rows=256)

    def conv_state(s, b):
        return _from_rows(_from_slabs(s), b)[None]

    return (_from_rows(yp, bsz), _from_rows(ys, dbsz),
            h_p[None], conv_state(rgb_p, bsz), conv_state(cvb_p, bsz),
            h_s[None], conv_state(rgb_s, dbsz), conv_state(cvb_s, dbsz))
```

```python
import functools

import jax
import jax.numpy as jnp
from jax.experimental import pallas as pl
from jax.experimental.pallas import tpu as pltpu

D_MODEL = 1024
N_META = 16
D_RG = 1024
N_RG_HEADS = 16
RG_HEAD_DIM = D_RG // N_RG_HEADS
RG_CONV = 4
RG_C = 8.0
D_CV = 1024
CV_CONV = 31
EPS = 1e-6

RG_HIST = RG_CONV - 1
CV_HIST = CV_CONV - 1
SUBLANES, LANES = 8, 128
PACKED_ROWS = 16
PACK_BLOCK_ROWS = 256
PACK_VMEM_LIMIT_BYTES = 32 * 1024 * 1024
MXU_TILE = 256
N_COL_TILES = D_RG // MXU_TILE
LANES_PER_TILE = MXU_TILE // LANES
N_LANE_GROUPS = D_CV // LANES
N_IN_SECTIONS = 5
SEC_XR, SEC_GR, SEC_CVV, SEC_CVG, SEC_GC = range(N_IN_SECTIONS)
CHUNK_ROWS = 128
CONV_TAP_GROUP = 4
VMEM_LIMIT_BYTES = 58 * 1024 * 1024

V_PRE_G, V_RG_CB, V_BA, V_BX, V_LAM, V_LN_G, V_LN_B, V_B2, V_POST_G, V_RG_CW = range(10)
N_VEC_ROWS = 16
CVW_ROWS = 32


def _silu(x):
    return x * jax.nn.sigmoid(x)


def _bf16_rows(w_u32):
    return pltpu.bitcast(w_u32, jnp.bfloat16)


def _pack_rows_kernel(w_ref, o_ref):
    packed = pltpu.bitcast(w_ref[...].astype(jnp.bfloat16), jnp.uint32)
    if len(o_ref.shape) == 2:
        o_ref[...] = packed
    else:
        tile_cols = o_ref.shape[-1]
        for t in range(o_ref.shape[0]):
            o_ref[t] = packed[:, t * tile_cols:(t + 1) * tile_cols]


def _pack_rows(w, tile_cols=None):
    k, n = w.shape
    rows = min(k, PACK_BLOCK_ROWS)
    assert k % rows == 0 and rows % (2 * PACKED_ROWS) == 0
    if tile_cols is None:
        out_spec = pl.BlockSpec((rows // 2, n), lambda i: (i, 0))
        out_shape = jax.ShapeDtypeStruct((k // 2, n), jnp.uint32)
    else:
        assert n % tile_cols == 0
        out_spec = pl.BlockSpec((n // tile_cols, rows // 2, tile_cols), lambda i: (0, i, 0))
        out_shape = jax.ShapeDtypeStruct((n // tile_cols, k // 2, tile_cols), jnp.uint32)
    return pl.pallas_call(
        _pack_rows_kernel,
        grid=(k // rows,),
        in_specs=[pl.BlockSpec((rows, n), lambda i: (i, 0))],
        out_specs=out_spec,
        out_shape=out_shape,
        compiler_params=pltpu.CompilerParams(vmem_limit_bytes=PACK_VMEM_LIMIT_BYTES),
    )(w)


def _decay(vec_ref):
    lam = vec_ref[V_LAM:V_LAM + 1, :]
    return RG_C * (jnp.maximum(-lam, 0.0) + jnp.log1p(jnp.exp(-jnp.abs(lam))))


def _row_sum(parts):
    s = parts[0]
    for p in parts[1:]:
        s = s + p
    return jnp.sum(s, axis=-1, keepdims=True)


def _load_rows(ref, c, R, B):
    if len(ref.shape) == 2:
        return ref[pl.ds(pl.multiple_of(c * R, R), R), :]
    t0 = pl.multiple_of(c * (R // B), R // B)
    xb = ref[:, pl.ds(t0, R // B), :]
    return jnp.swapaxes(xb, 0, 1).reshape(R, xb.shape[-1])


def _store_rows(ref, c, R, B, val):
    if len(ref.shape) == 2:
        ref[pl.ds(pl.multiple_of(c * R, R), R), :] = val
    else:
        t0 = pl.multiple_of(c * (R // B), R // B)
        ref[:, pl.ds(t0, R // B), :] = jnp.swapaxes(val.reshape(R // B, B, val.shape[-1]), 0, 1)


def _chunk_body(c, *, R, B, decay, x_ref, y_ref, vec_ref, cvw_ref, w_in_ref, wg_ref, pw2_ref, w_out_ref,
                x_scr, xn_scr, cv_scr, xr_ext, h_scr, v_ext, vb_ext):
    chunk_t = R // B
    r0 = pl.multiple_of(c * R, R)
    n_ptiles = R // PACKED_ROWS
    pstep = B // PACKED_ROWS

    def vec(row, cols):
        return vec_ref[row:row + 1, cols]

    x = _load_rows(x_ref, c, R, B)
    x_scr[...] = x
    xn = x * jax.lax.rsqrt(jnp.mean(x * x, axis=-1, keepdims=True) + EPS) * vec_ref[V_PRE_G:V_PRE_G + 1, :]
    xn_scr[...] = xn.astype(jnp.bfloat16)

    def in_proj_tile(sec, g):
        return jnp.dot(xn_scr[...], _bf16_rows(w_in_ref[sec * N_COL_TILES + g]), preferred_element_type=jnp.float32)

    for g in range(N_COL_TILES):
        v = in_proj_tile(SEC_CVV, g) * jax.nn.sigmoid(in_proj_tile(SEC_CVG, g))
        vb = v.astype(jnp.bfloat16)
        for half in range(LANES_PER_TILE):
            j = g * LANES_PER_TILE + half
            lcols = slice(half * LANES, (half + 1) * LANES)
            v_ext[j, pl.ds(pl.multiple_of(CV_HIST * B + r0, SUBLANES), R), :] = v[:, lcols]
            vb_ext[j, pl.ds(pl.multiple_of(CV_HIST * B + r0, PACKED_ROWS), R), :] = vb[:, lcols]

    def conv_lane_group(j, carry):
        acc = [jnp.broadcast_to(cvw_ref[j, CV_CONV:CV_CONV + 1, :], (PACKED_ROWS, LANES))] * n_ptiles
        for k0 in range(0, CV_CONV, CONV_TAP_GROUP):
            taps = range(k0, min(k0 + CONV_TAP_GROUP, CV_CONV))
            w = {k: jnp.broadcast_to(cvw_ref[j, k:k + 1, :], (PACKED_ROWS, LANES)).astype(jnp.bfloat16)
                 for k in taps}
            for i in range(n_ptiles):
                part = None
                for k in taps:
                    row = pl.multiple_of(r0 + (i + pstep * k) * PACKED_ROWS, PACKED_ROWS)
                    prod = w[k] * vb_ext[j, pl.ds(row, PACKED_ROWS), :]
                    part = prod if part is None else part + prod
                acc[i] = acc[i] + part.astype(jnp.float32)
        cv_scr[j] = jnp.concatenate(acc, axis=0)
        return carry

    jax.lax.fori_loop(0, N_LANE_GROUPS, conv_lane_group, 0)

    cv = [cv_scr[j] for j in range(N_LANE_GROUPS)]
    inv_d = 1.0 / D_CV
    mu = _row_sum(cv) * inv_d
    cvc = [p - mu for p in cv]
    var = _row_sum([p * p for p in cvc]) * inv_d
    rs = jax.lax.rsqrt(var + EPS)
    s_act = []
    for j in range(N_LANE_GROUPS):
        lcols = slice(j * LANES, (j + 1) * LANES)
        ln = cvc[j] * rs * vec(V_LN_G, lcols) + vec(V_LN_B, lcols)
        s_act.append(_silu(ln).astype(jnp.bfloat16))
    s_act = jnp.concatenate(s_act, axis=-1)

    for g in range(N_COL_TILES):
        u = in_proj_tile(SEC_XR, g)
        for half in range(LANES_PER_TILE):
            xr_ext[g * LANES_PER_TILE + half, pl.ds(pl.multiple_of(RG_HIST * B + r0, 8), R), :] = (
                u[:, half * LANES:(half + 1) * LANES])
    y_r = []
    for g in range(N_COL_TILES):
        cols = slice(g * MXU_TILE, (g + 1) * MXU_TILE)
        sgr = _silu(in_proj_tile(SEC_GR, g))
        halves = []
        for half in range(LANES_PER_TILE):
            j = g * LANES_PER_TILE + half
            lcols = slice(j * LANES, (j + 1) * LANES)
            acc = vec(V_RG_CB, lcols)
            for k in range(RG_CONV):
                tap = xr_ext[j, pl.ds(pl.multiple_of(k * B + r0, 8), R), :]
                acc = acc + vec(V_RG_CW + k, lcols) * tap
            halves.append(acc)
        xc = jnp.concatenate(halves, axis=-1)
        z = jnp.dot(xc.astype(jnp.bfloat16), _bf16_rows(wg_ref[g]), preferred_element_type=jnp.float32)
        r_gate = jax.nn.sigmoid(z[:, :MXU_TILE] + vec(V_BA, cols))
        i_gate = jax.nn.sigmoid(z[:, MXU_TILE:] + vec(V_BX, cols))
        neg_log_a = r_gate * decay[:, cols]
        a = jnp.exp(-neg_log_a)
        one_minus_a2 = jnp.tanh(neg_log_a) * (a * a + 1.0)
        bx = jnp.sqrt(one_minus_a2) * (i_gate * xc)
        h = h_scr[:, cols]
        hs = []
        for s in range(chunk_t):
            h = a[s * B:(s + 1) * B, :] * h + bx[s * B:(s + 1) * B, :]
            hs.append(h)
        h_scr[:, cols] = h
        y_r.append((jnp.concatenate(hs, axis=0) * sgr).astype(jnp.bfloat16))
    y_r = jnp.concatenate(y_r, axis=-1)

    z_c = jnp.dot(s_act, _bf16_rows(pw2_ref[...]), preferred_element_type=jnp.float32)
    sgc = jnp.concatenate([_silu(in_proj_tile(SEC_GC, g)) for g in range(N_COL_TILES)], axis=-1)
    y_c = ((z_c + vec_ref[V_B2:V_B2 + 1, :]) * sgc).astype(jnp.bfloat16)
    y = jnp.dot(jnp.concatenate([y_r, y_c], axis=-1), _bf16_rows(w_out_ref[...]),
                preferred_element_type=jnp.float32)
    yn = y * jax.lax.rsqrt(jnp.mean(y * y, axis=-1, keepdims=True) + EPS) * vec_ref[V_POST_G:V_POST_G + 1, :]
    _store_rows(y_ref, c, R, B, x_scr[...] + yn)


def _layer_kernel(x_ref, h0_ref, rgb0_ref, cvb0_ref, vec_ref, cvw_ref, w_in_ref, wg_ref, pw2_ref, w_out_ref,
                  y_ref, h_out_ref, rgb_out_ref, cvb_out_ref,
                  x_scr, xn_scr, cv_scr, xr_ext, h_scr, v_ext, vb_ext, *, batch, block_t, chunk_t, n_steps):
    B = batch
    R = chunk_t * B
    blk = block_t * B
    step = pl.program_id(0)

    @pl.when(step == 0)
    def _init():
        xr_ext[:, 0:RG_HIST * B, :] = rgb0_ref[...]
        v_ext[:, 0:CV_HIST * B, :] = cvb0_ref[...]
        vb_ext[:, 0:CV_HIST * B, :] = cvb0_ref[...].astype(vb_ext.dtype)
        h_scr[...] = h0_ref[...]

    decay = _decay(vec_ref)

    def chunk(c, carry):
        _chunk_body(c, R=R, B=B, decay=decay, x_ref=x_ref, y_ref=y_ref, vec_ref=vec_ref,
                    cvw_ref=cvw_ref, w_in_ref=w_in_ref, wg_ref=wg_ref, pw2_ref=pw2_ref, w_out_ref=w_out_ref,
                    x_scr=x_scr, xn_scr=xn_scr, cv_scr=cv_scr, xr_ext=xr_ext, h_scr=h_scr, v_ext=v_ext, vb_ext=vb_ext)
        return carry

    jax.lax.fori_loop(0, block_t // chunk_t, chunk, 0)

    @pl.when(step == n_steps - 1)
    def _emit_state():
        h_out_ref[...] = h_scr[...]
        rgb_out_ref[...] = xr_ext[:, blk:blk + RG_HIST * B, :]
        cvb_out_ref[...] = v_ext[:, blk:blk + CV_HIST * B, :]

    if n_steps > 1:
        @pl.when(step < n_steps - 1)
        def _carry_hist():
            xr_ext[:, 0:RG_HIST * B, :] = xr_ext[:, blk:blk + RG_HIST * B, :]
            v_ext[:, 0:CV_HIST * B, :] = v_ext[:, blk:blk + CV_HIST * B, :]
            vb_ext[:, 0:CV_HIST * B, :] = vb_ext[:, blk:blk + CV_HIST * B, :]


def _to_slabs(s):
    r, c = s.shape
    return s.reshape(r, c // LANES, LANES).transpose(1, 0, 2)


def _from_slabs(s):
    g, r, l = s.shape
    return s.transpose(1, 0, 2).reshape(r, g * l)


def _stream_layer(x_rows, h0, rgb0, cvb0, weights, *, batch, n_t, block_t, chunk_rows=CHUNK_ROWS):
    B = batch
    assert B % PACKED_ROWS == 0 and chunk_rows % B == 0
    chunk_t = chunk_rows // B
    assert n_t % block_t == 0 and block_t % chunk_t == 0
    n_steps = n_t // block_t
    assert n_steps == 1 or block_t >= CV_HIST
    blk = block_t * B
    f32 = jnp.float32

    def full(a):
        return pl.BlockSpec(a.shape, lambda i: (0,) * a.ndim)

    kern = functools.partial(_layer_kernel, batch=B, block_t=block_t, chunk_t=chunk_t, n_steps=n_steps)
    consts = (h0, rgb0, cvb0) + tuple(weights)
    if x_rows.ndim == 2:
        xy_spec = pl.BlockSpec((blk, D_MODEL), lambda i: (i, 0))
    else:
        assert x_rows.shape == (B, n_t, D_MODEL)
        xy_spec = pl.BlockSpec((B, block_t, D_MODEL), lambda i: (0, i, 0))
    return pl.pallas_call(
        kern,
        grid=(n_steps,),
        in_specs=[xy_spec] + [full(a) for a in consts],
        out_specs=[xy_spec,
                   pl.BlockSpec((B, D_RG), lambda i: (0, 0)),
                   pl.BlockSpec((N_LANE_GROUPS, RG_HIST * B, LANES), lambda i: (0, 0, 0)),
                   pl.BlockSpec((N_LANE_GROUPS, CV_HIST * B, LANES), lambda i: (0, 0, 0))],
        out_shape=[jax.ShapeDtypeStruct(x_rows.shape, f32),
                   jax.ShapeDtypeStruct((B, D_RG), f32),
                   jax.ShapeDtypeStruct((N_LANE_GROUPS, RG_HIST * B, LANES), f32),
                   jax.ShapeDtypeStruct((N_LANE_GROUPS, CV_HIST * B, LANES), f32)],
        scratch_shapes=[pltpu.VMEM((chunk_rows, D_MODEL), f32),
                        pltpu.VMEM((chunk_rows, D_MODEL), jnp.bfloat16),
                        pltpu.VMEM((N_LANE_GROUPS, chunk_rows, LANES), f32),
                        pltpu.VMEM((N_LANE_GROUPS, (RG_HIST + block_t) * B, LANES), f32),
                        pltpu.VMEM((B, D_RG), f32),
                        pltpu.VMEM((N_LANE_GROUPS, (CV_HIST + block_t) * B, LANES), f32),
                        pltpu.VMEM((N_LANE_GROUPS, (CV_HIST + block_t) * B, LANES), jnp.bfloat16)],
        compiler_params=pltpu.CompilerParams(dimension_semantics=("arbitrary",),
                                             vmem_limit_bytes=VMEM_LIMIT_BYTES),
    )(x_rows, *consts)


def _block_diag_gates(wa, wx):
    hpg = MXU_TILE // RG_HEAD_DIM
    eye = jnp.eye(hpg, dtype=wa.dtype)

    def bd(w):
        w4 = w.reshape(N_COL_TILES, hpg, RG_HEAD_DIM, RG_HEAD_DIM)
        return jnp.einsum('ghij,hk->ghikj', w4, eye).reshape(N_COL_TILES, MXU_TILE, MXU_TILE)

    return jnp.concatenate([bd(wa), bd(wx)], axis=-1)


def _to_rows(x):
    b, t, c = x.shape
    return x.transpose(1, 0, 2).reshape(t * b, c)


def _from_rows(x, b):
    tb, c = x.shape
    return x.reshape(tb // b, b, c).transpose(1, 0, 2)


def kernel(x_prompt, x_sample, state_rg_h, state_rg_conv, state_cv_conv, meta_tokens, pre_norm_g, w_in,
           rg_conv_w, rg_conv_b, rg_wa, rg_ba, rg_wx, rg_bx, rg_lambda, cv_conv_w, cv_conv_b, cv_ln_g,
           cv_ln_b, cv_w_pw2, cv_b_pw2, w_out, post_norm_g):
    assert w_in.shape[0] == 1, "single layer"
    f32 = jnp.float32
    bsz, seq, _ = x_prompt.shape
    dbsz, dseq, _ = x_sample.shape

    rows = [pre_norm_g[0], rg_conv_b[0], rg_ba[0], rg_bx[0], rg_lambda[0], cv_ln_g[0], cv_ln_b[0],
            cv_b_pw2[0], post_norm_g[0]]
    vecs = jnp.concatenate([jnp.stack(rows).astype(f32), rg_conv_w[0].astype(f32),
                            jnp.zeros((N_VEC_ROWS - len(rows) - RG_CONV, D_MODEL), f32)], axis=0)
    cvw = _to_slabs(jnp.concatenate([cv_conv_w[0].astype(f32), cv_conv_b[0].astype(f32)[None]], axis=0))
    wg = _block_diag_gates(rg_wa[0], rg_wx[0])
    wg = _pack_rows(wg.reshape(N_COL_TILES * MXU_TILE, 2 * MXU_TILE)).reshape(N_COL_TILES, MXU_TILE // 2, 2 * MXU_TILE)
    w_in_tiles = _pack_rows(w_in[0], tile_cols=MXU_TILE)
    weights = (vecs, cvw, w_in_tiles, wg, _pack_rows(cv_w_pw2[0]), _pack_rows(w_out[0]))

    mb = PACKED_ROWS
    xm = jnp.broadcast_to(meta_tokens.astype(f32)[:, None, :], (N_META, mb, D_MODEL)).reshape(N_META * mb, D_MODEL)
    _, h_m, rgb_m, cvb_m = _stream_layer(
        xm, jnp.zeros((mb, D_RG), f32), jnp.zeros((N_LANE_GROUPS, RG_HIST * mb, LANES), f32),
        jnp.zeros((N_LANE_GROUPS, CV_HIST * mb, LANES), f32), weights, batch=mb, n_t=N_META, block_t=N_META)

    def rebatch(s, n_hist):
        lead = s.shape[:-2]
        s = s.reshape(*lead, n_hist, mb, s.shape[-1])[..., :1, :]
        return jnp.broadcast_to(s, (*lead, n_hist, bsz, s.shape[-1])).reshape(*lead, n_hist * bsz, s.shape[-1])

    yp, h_p, rgb_p, cvb_p = _stream_layer(
        x_prompt, rebatch(h_m, 1), rebatch(rgb_m, RG_HIST), rebatch(cvb_m, CV_HIST),
        weights, batch=bsz, n_t=seq, block_t=32, chunk_rows=256)

    ys, h_s, rgb_s, cvb_s = _stream_layer(
        _to_rows(x_sample), state_rg_h[0], _to_slabs(_to_rows(state_rg_conv[0])),
        _to_slabs(_to_rows(state_cv_conv[0])), weights, batch=dbsz, n_t=dseq, block_t=dseq, chunk_rows=256)

    def conv_state(s, b):
        return _from_rows(_from_slabs(s), b)[None]

    return (yp, _from_rows(ys, dbsz),
            h_p[None], conv_state(rgb_p, bsz), conv_state(cvb_p, bsz),
            h_s[None], conv_state(rgb_s, dbsz), conv_state(cvb_s, dbsz))
```

```python
import functools

import jax
import jax.numpy as jnp
from jax.experimental import pallas as pl
from jax.experimental.pallas import tpu as pltpu

D_MODEL = 1024
N_META = 16
D_RG = 1024
N_RG_HEADS = 16
RG_HEAD_DIM = D_RG // N_RG_HEADS
RG_CONV = 4
RG_C = 8.0
D_CV = 1024
CV_CONV = 31
EPS = 1e-6

RG_HIST = RG_CONV - 1
CV_HIST = CV_CONV - 1
SUBLANES, LANES = 8, 128
PACKED_ROWS = 16
PACK_BLOCK_ROWS = 256
PACK_VMEM_LIMIT_BYTES = 32 * 1024 * 1024
MXU_TILE = 256
N_COL_TILES = D_RG // MXU_TILE
LANES_PER_TILE = MXU_TILE // LANES
N_LANE_GROUPS = D_CV // LANES
N_IN_SECTIONS = 5
SEC_XR, SEC_GR, SEC_CVV, SEC_CVG, SEC_GC = range(N_IN_SECTIONS)
CHUNK_ROWS = 128
CONV_TAP_GROUP = 4
VMEM_LIMIT_BYTES = 58 * 1024 * 1024

V_PRE_G, V_RG_CB, V_BA, V_BX, V_LAM, V_LN_G, V_LN_B, V_B2, V_POST_G, V_RG_CW = range(10)
N_VEC_ROWS = 16
CVW_ROWS = 32


def _silu(x):
    return x * jax.nn.sigmoid(x)


def _bf16_rows(w_u32):
    return pltpu.bitcast(w_u32, jnp.bfloat16)


def _pack_rows_kernel(w_ref, o_ref):
    packed = pltpu.bitcast(w_ref[...].astype(jnp.bfloat16), jnp.uint32)
    if len(o_ref.shape) == 2:
        o_ref[...] = packed
    else:
        tile_cols = o_ref.shape[-1]
        for t in range(o_ref.shape[0]):
            o_ref[t] = packed[:, t * tile_cols:(t + 1) * tile_cols]


def _pack_rows(w, tile_cols=None):
    k, n = w.shape
    rows = min(k, PACK_BLOCK_ROWS)
    assert k % rows == 0 and rows % (2 * PACKED_ROWS) == 0
    if tile_cols is None:
        out_spec = pl.BlockSpec((rows // 2, n), lambda i: (i, 0))
        out_shape = jax.ShapeDtypeStruct((k // 2, n), jnp.uint32)
    else:
        assert n % tile_cols == 0
        out_spec = pl.BlockSpec((n // tile_cols, rows // 2, tile_cols), lambda i: (0, i, 0))
        out_shape = jax.ShapeDtypeStruct((n // tile_cols, k // 2, tile_cols), jnp.uint32)
    return pl.pallas_call(
        _pack_rows_kernel,
        grid=(k // rows,),
        in_specs=[pl.BlockSpec((rows, n), lambda i: (i, 0))],
        out_specs=out_spec,
        out_shape=out_shape,
        compiler_params=pltpu.CompilerParams(vmem_limit_bytes=PACK_VMEM_LIMIT_BYTES),
    )(w)


def _decay(vec_ref):
    lam = vec_ref[V_LAM:V_LAM + 1, :]
    return RG_C * (jnp.maximum(-lam, 0.0) + jnp.log1p(jnp.exp(-jnp.abs(lam))))


def _row_sum(parts):
    s = parts[0]
    for p in parts[1:]:
        s = s + p
    return jnp.sum(s, axis=-1, keepdims=True)


def _load_rows(ref, c, R, B):
    if len(ref.shape) == 2:
        return ref[pl.ds(pl.multiple_of(c * R, R), R), :]
    t0 = pl.multiple_of(c * (R // B), R // B)
    xb = ref[:, pl.ds(t0, R // B), :]
    return jnp.swapaxes(xb, 0, 1).reshape(R, xb.shape[-1])


def _store_rows(ref, c, R, B, val):
    if len(ref.shape) == 2:
        ref[pl.ds(pl.multiple_of(c * R, R), R), :] = val
    else:
        t0 = pl.multiple_of(c * (R // B), R // B)
        ref[:, pl.ds(t0, R // B), :] = jnp.swapaxes(val.reshape(R // B, B, val.shape[-1]), 0, 1)


def _chunk_body(c, *, R, B, decay, x_ref, y_ref, vec_ref, cvw_ref, w_in_ref, wg_ref, pw2_ref, w_out_ref,
                x_scr, xn_scr, cv_scr, xr_ext, h_scr, v_ext, vb_ext):
    chunk_t = R // B
    r0 = pl.multiple_of(c * R, R)
    n_ptiles = R // PACKED_ROWS
    pstep = B // PACKED_ROWS

    def vec(row, cols):
        return vec_ref[row:row + 1, cols]

    x = _load_rows(x_ref, c, R, B)
    x_scr[...] = x
    xn = x * jax.lax.rsqrt(jnp.mean(x * x, axis=-1, keepdims=True) + EPS) * vec_ref[V_PRE_G:V_PRE_G + 1, :]
    xn_scr[...] = xn.astype(jnp.bfloat16)

    def in_proj_tile(sec, g):
        return jnp.dot(xn_scr[...], _bf16_rows(w_in_ref[sec * N_COL_TILES + g]), preferred_element_type=jnp.float32)

    for g in range(N_COL_TILES):
        v = in_proj_tile(SEC_CVV, g) * jax.nn.sigmoid(in_proj_tile(SEC_CVG, g))
        vb = v.astype(jnp.bfloat16)
        for half in range(LANES_PER_TILE):
            j = g * LANES_PER_TILE + half
            lcols = slice(half * LANES, (half + 1) * LANES)
            v_ext[j, pl.ds(pl.multiple_of(CV_HIST * B + r0, SUBLANES), R), :] = v[:, lcols]
            vb_ext[j, pl.ds(pl.multiple_of(CV_HIST * B + r0, PACKED_ROWS), R), :] = vb[:, lcols]

    def conv_lane_group(j, carry):
        acc = [jnp.broadcast_to(cvw_ref[j, CV_CONV:CV_CONV + 1, :], (PACKED_ROWS, LANES))] * n_ptiles
        for k0 in range(0, CV_CONV, CONV_TAP_GROUP):
            taps = range(k0, min(k0 + CONV_TAP_GROUP, CV_CONV))
            w = {k: jnp.broadcast_to(cvw_ref[j, k:k + 1, :], (PACKED_ROWS, LANES)).astype(jnp.bfloat16)
                 for k in taps}
            for i in range(n_ptiles):
                part = None
                for k in taps:
                    row = pl.multiple_of(r0 + (i + pstep * k) * PACKED_ROWS, PACKED_ROWS)
                    prod = w[k] * vb_ext[j, pl.ds(row, PACKED_ROWS), :]
                    part = prod if part is None else part + prod
                acc[i] = acc[i] + part.astype(jnp.float32)
        cv_scr[j] = jnp.concatenate(acc, axis=0)
        return carry

    jax.lax.fori_loop(0, N_LANE_GROUPS, conv_lane_group, 0)

    cv = [cv_scr[j] for j in range(N_LANE_GROUPS)]
    inv_d = 1.0 / D_CV
    mu = _row_sum(cv) * inv_d
    cvc = [p - mu for p in cv]
    var = _row_sum([p * p for p in cvc]) * inv_d
    rs = jax.lax.rsqrt(var + EPS)
    s_act = []
    for j in range(N_LANE_GROUPS):
        lcols = slice(j * LANES, (j + 1) * LANES)
        ln = cvc[j] * rs * vec(V_LN_G, lcols) + vec(V_LN_B, lcols)
        s_act.append(_silu(ln).astype(jnp.bfloat16))
    s_act = jnp.concatenate(s_act, axis=-1)

    for g in range(N_COL_TILES):
        u = in_proj_tile(SEC_XR, g)
        for half in range(LANES_PER_TILE):
            xr_ext[g * LANES_PER_TILE + half, pl.ds(pl.multiple_of(RG_HIST * B + r0, 8), R), :] = (
                u[:, half * LANES:(half + 1) * LANES])
    y_r = []
    for g in range(N_COL_TILES):
        cols = slice(g * MXU_TILE, (g + 1) * MXU_TILE)
        sgr = _silu(in_proj_tile(SEC_GR, g))
        halves = []
        for half in range(LANES_PER_TILE):
            j = g * LANES_PER_TILE + half
            lcols = slice(j * LANES, (j + 1) * LANES)
            acc = vec(V_RG_CB, lcols)
            for k in range(RG_CONV):
                tap = xr_ext[j, pl.ds(pl.multiple_of(k * B + r0, 8), R), :]
                acc = acc + vec(V_RG_CW + k, lcols) * tap
            halves.append(acc)
        xc = jnp.concatenate(halves, axis=-1)
        z = jnp.dot(xc.astype(jnp.bfloat16), _bf16_rows(wg_ref[g]), preferred_element_type=jnp.float32)
        r_gate = jax.nn.sigmoid(z[:, :MXU_TILE] + vec(V_BA, cols))
        i_gate = jax.nn.sigmoid(z[:, MXU_TILE:] + vec(V_BX, cols))
        neg_log_a = r_gate * decay[:, cols]
        a = jnp.exp(-neg_log_a)
        one_minus_a2 = jnp.tanh(neg_log_a) * (a * a + 1.0)
        bx = jnp.sqrt(one_minus_a2) * (i_gate * xc)
        h = h_scr[:, cols]
        hs = []
        for s in range(chunk_t):
            h = a[s * B:(s + 1) * B, :] * h + bx[s * B:(s + 1) * B, :]
            hs.append(h)
        h_scr[:, cols] = h
        y_r.append((jnp.concatenate(hs, axis=0) * sgr).astype(jnp.bfloat16))
    y_r = jnp.concatenate(y_r, axis=-1)

    z_c = jnp.dot(s_act, _bf16_rows(pw2_ref[...]), preferred_element_type=jnp.float32)
    sgc = jnp.concatenate([_silu(in_proj_tile(SEC_GC, g)) for g in range(N_COL_TILES)], axis=-1)
    y_c = ((z_c + vec_ref[V_B2:V_B2 + 1, :]) * sgc).astype(jnp.bfloat16)
    y = jnp.dot(jnp.concatenate([y_r, y_c], axis=-1), _bf16_rows(w_out_ref[...]),
                preferred_element_type=jnp.float32)
    yn = y * jax.lax.rsqrt(jnp.mean(y * y, axis=-1, keepdims=True) + EPS) * vec_ref[V_POST_G:V_POST_G + 1, :]
    _store_rows(y_ref, c, R, B, x_scr[...] + yn)


def _layer_kernel(x_ref, h0_ref, rgb0_ref, cvb0_ref, vec_ref, cvw_ref, w_in_ref, wg_ref, pw2_ref, w_out_ref,
                  y_ref, h_out_ref, rgb_out_ref, cvb_out_ref,
                  x_scr, xn_scr, cv_scr, xr_ext, h_scr, v_ext, vb_ext, *, batch, block_t, chunk_t, n_steps):
    B = batch
    R = chunk_t * B
    blk = block_t * B
    step = pl.program_id(0)

    @pl.when(step == 0)
    def _init():
        for j in range(N_LANE_GROUPS):
            lcols = slice(j * LANES, (j + 1) * LANES)
            xr_ext[j, 0:RG_HIST * B, :] = rgb0_ref[:, lcols]
            v_ext[j, 0:CV_HIST * B, :] = cvb0_ref[:, lcols]
            vb_ext[j, 0:CV_HIST * B, :] = cvb0_ref[:, lcols].astype(vb_ext.dtype)
        h_scr[...] = h0_ref[...]

    decay = _decay(vec_ref)

    def chunk(c, carry):
        _chunk_body(c, R=R, B=B, decay=decay, x_ref=x_ref, y_ref=y_ref, vec_ref=vec_ref,
                    cvw_ref=cvw_ref, w_in_ref=w_in_ref, wg_ref=wg_ref, pw2_ref=pw2_ref, w_out_ref=w_out_ref,
                    x_scr=x_scr, xn_scr=xn_scr, cv_scr=cv_scr, xr_ext=xr_ext, h_scr=h_scr, v_ext=v_ext, vb_ext=vb_ext)
        return carry

    jax.lax.fori_loop(0, block_t // chunk_t, chunk, 0)

    @pl.when(step == n_steps - 1)
    def _emit_state():
        h_out_ref[...] = h_scr[...]
        for j in range(N_LANE_GROUPS):
            lcols = slice(j * LANES, (j + 1) * LANES)
            rgb_out_ref[:, lcols] = xr_ext[j, blk:blk + RG_HIST * B, :]
            cvb_out_ref[:, lcols] = v_ext[j, blk:blk + CV_HIST * B, :]

    if n_steps > 1:
        @pl.when(step < n_steps - 1)
        def _carry_hist():
            xr_ext[:, 0:RG_HIST * B, :] = xr_ext[:, blk:blk + RG_HIST * B, :]
            v_ext[:, 0:CV_HIST * B, :] = v_ext[:, blk:blk + CV_HIST * B, :]
            vb_ext[:, 0:CV_HIST * B, :] = vb_ext[:, blk:blk + CV_HIST * B, :]


def _to_slabs(s):
    r, c = s.shape
    return s.reshape(r, c // LANES, LANES).transpose(1, 0, 2)


def _stream_layer(x_rows, h0, rgb0, cvb0, weights, *, batch, n_t, block_t, chunk_rows=CHUNK_ROWS):
    B = batch
    assert B % PACKED_ROWS == 0 and chunk_rows % B == 0
    chunk_t = chunk_rows // B
    assert n_t % block_t == 0 and block_t % chunk_t == 0
    n_steps = n_t // block_t
    assert n_steps == 1 or block_t >= CV_HIST
    blk = block_t * B
    f32 = jnp.float32

    def full(a):
        return pl.BlockSpec(a.shape, lambda i: (0,) * a.ndim)

    kern = functools.partial(_layer_kernel, batch=B, block_t=block_t, chunk_t=chunk_t, n_steps=n_steps)
    consts = (h0, rgb0, cvb0) + tuple(weights)
    if x_rows.ndim == 2:
        xy_spec = pl.BlockSpec((blk, D_MODEL), lambda i: (i, 0))
    else:
        assert x_rows.shape == (B, n_t, D_MODEL)
        xy_spec = pl.BlockSpec((B, block_t, D_MODEL), lambda i: (0, i, 0))
    return pl.pallas_call(
        kern,
        grid=(n_steps,),
        in_specs=[xy_spec] + [full(a) for a in consts],
        out_specs=[xy_spec,
                   pl.BlockSpec((B, D_RG), lambda i: (0, 0)),
                   pl.BlockSpec((RG_HIST * B, D_RG), lambda i: (0, 0)),
                   pl.BlockSpec((CV_HIST * B, D_CV), lambda i: (0, 0))],
        out_shape=[jax.ShapeDtypeStruct(x_rows.shape, f32),
                   jax.ShapeDtypeStruct((B, D_RG), f32),
                   jax.ShapeDtypeStruct((RG_HIST * B, D_RG), f32),
                   jax.ShapeDtypeStruct((CV_HIST * B, D_CV), f32)],
        scratch_shapes=[pltpu.VMEM((chunk_rows, D_MODEL), f32),
                        pltpu.VMEM((chunk_rows, D_MODEL), jnp.bfloat16),
                        pltpu.VMEM((N_LANE_GROUPS, chunk_rows, LANES), f32),
                        pltpu.VMEM((N_LANE_GROUPS, (RG_HIST + block_t) * B, LANES), f32),
                        pltpu.VMEM((B, D_RG), f32),
                        pltpu.VMEM((N_LANE_GROUPS, (CV_HIST + block_t) * B, LANES), f32),
                        pltpu.VMEM((N_LANE_GROUPS, (CV_HIST + block_t) * B, LANES), jnp.bfloat16)],
        compiler_params=pltpu.CompilerParams(dimension_semantics=("arbitrary",),
                                             vmem_limit_bytes=VMEM_LIMIT_BYTES),
    )(x_rows, *consts)


def _block_diag_gates(wa, wx):
    hpg = MXU_TILE // RG_HEAD_DIM
    eye = jnp.eye(hpg, dtype=wa.dtype)

    def bd(w):
        w4 = w.reshape(N_COL_TILES, hpg, RG_HEAD_DIM, RG_HEAD_DIM)
        return jnp.einsum('ghij,hk->ghikj', w4, eye).reshape(N_COL_TILES, MXU_TILE, MXU_TILE)

    return jnp.concatenate([bd(wa), bd(wx)], axis=-1)


def _to_rows(x):
    b, t, c = x.shape
    return x.transpose(1, 0, 2).reshape(t * b, c)


def _from_rows(x, b):
    tb, c = x.shape
    return x.reshape(tb // b, b, c).transpose(1, 0, 2)


def kernel(x_prompt, x_sample, state_rg_h, state_rg_conv, state_cv_conv, meta_tokens, pre_norm_g, w_in,
           rg_conv_w, rg_conv_b, rg_wa, rg_ba, rg_wx, rg_bx, rg_lambda, cv_conv_w, cv_conv_b, cv_ln_g,
           cv_ln_b, cv_w_pw2, cv_b_pw2, w_out, post_norm_g):
    assert w_in.shape[0] == 1, "single layer"
    f32 = jnp.float32
    bsz, seq, _ = x_prompt.shape
    dbsz, dseq, _ = x_sample.shape

    rows = [pre_norm_g[0], rg_conv_b[0], rg_ba[0], rg_bx[0], rg_lambda[0], cv_ln_g[0], cv_ln_b[0],
            cv_b_pw2[0], post_norm_g[0]]
    vecs = jnp.concatenate([jnp.stack(rows).astype(f32), rg_conv_w[0].astype(f32),
                            jnp.zeros((N_VEC_ROWS - len(rows) - RG_CONV, D_MODEL), f32)], axis=0)
    cvw = _to_slabs(jnp.concatenate([cv_conv_w[0].astype(f32), cv_conv_b[0].astype(f32)[None]], axis=0))
    wg = _block_diag_gates(rg_wa[0], rg_wx[0])
    wg = _pack_rows(wg.reshape(N_COL_TILES * MXU_TILE, 2 * MXU_TILE)).reshape(N_COL_TILES, MXU_TILE // 2, 2 * MXU_TILE)
    w_in_tiles = _pack_rows(w_in[0], tile_cols=MXU_TILE)
    weights = (vecs, cvw, w_in_tiles, wg, _pack_rows(cv_w_pw2[0]), _pack_rows(w_out[0]))

    mb = PACKED_ROWS
    xm = jnp.broadcast_to(meta_tokens.astype(f32)[:, None, :], (N_META, mb, D_MODEL)).reshape(N_META * mb, D_MODEL)
    _, h_m, rgb_m, cvb_m = _stream_layer(
        xm, jnp.zeros((mb, D_RG), f32), jnp.zeros((RG_HIST * mb, D_RG), f32),
        jnp.zeros((CV_HIST * mb, D_CV), f32), weights, batch=mb, n_t=N_META, block_t=N_META,
        chunk_rows=N_META * mb)

    def rebatch(s, n_hist):
        s = s.reshape(n_hist, mb, s.shape[-1])[:, :1, :]
        return jnp.broadcast_to(s, (n_hist, bsz, s.shape[-1])).reshape(n_hist * bsz, s.shape[-1])

    yp, h_p, rgb_p, cvb_p = _stream_layer(
        x_prompt, rebatch(h_m, 1), rebatch(rgb_m, RG_HIST), rebatch(cvb_m, CV_HIST),
        weights, batch=bsz, n_t=seq, block_t=32, chunk_rows=256)

    ys, h_s, rgb_s, cvb_s = _stream_layer(
        x_sample, state_rg_h[0], _to_rows(state_rg_conv[0]), _to_rows(state_cv_conv[0]),
        weights, batch=dbsz, n_t=dseq, block_t=dseq, chunk_rows=256)

    def conv_state(s, b):
        return _from_rows(s, b)[None]

    return (yp, ys,
            h_p[None], conv_state(rgb_p, bsz), conv_state(cvb_p, bsz),
            h_s[None], conv_state(rgb_s, dbsz), conv_state(cvb_s, dbsz))
```

```python
import functools

import jax
import jax.numpy as jnp
from jax.experimental import pallas as pl
from jax.experimental.pallas import tpu as pltpu

D_MODEL = 1024
N_META = 16
D_RG = 1024
N_RG_HEADS = 16
RG_HEAD_DIM = D_RG // N_RG_HEADS
RG_CONV = 4
RG_C = 8.0
D_CV = 1024
CV_CONV = 31
EPS = 1e-6

RG_HIST = RG_CONV - 1
CV_HIST = CV_CONV - 1
SUBLANES, LANES = 8, 128
PACKED_ROWS = 16
PACK_BLOCK_ROWS = 256
PACK_VMEM_LIMIT_BYTES = 32 * 1024 * 1024
MXU_TILE = 256
N_COL_TILES = D_RG // MXU_TILE
LANES_PER_TILE = MXU_TILE // LANES
N_LANE_GROUPS = D_CV // LANES
N_IN_SECTIONS = 5
SEC_XR, SEC_GR, SEC_CVV, SEC_CVG, SEC_GC = range(N_IN_SECTIONS)
CHUNK_ROWS = 128
CONV_TAP_GROUP = 4
VMEM_LIMIT_BYTES = 58 * 1024 * 1024

V_PRE_G, V_RG_CB, V_BA, V_BX, V_LAM, V_LN_G, V_LN_B, V_B2, V_POST_G, V_RG_CW = range(10)
N_VEC_ROWS = 16
CVW_ROWS = 32


def _silu(x):
    return x * jax.nn.sigmoid(x)


def _bf16_rows(w_u32):
    return pltpu.bitcast(w_u32, jnp.bfloat16)


def _pack_rows_kernel(w_ref, o_ref):
    packed = pltpu.bitcast(w_ref[...].astype(jnp.bfloat16), jnp.uint32)
    if len(o_ref.shape) == 2:
        o_ref[...] = packed
    else:
        tile_cols = o_ref.shape[-1]
        for t in range(o_ref.shape[0]):
            o_ref[t] = packed[:, t * tile_cols:(t + 1) * tile_cols]


def _pack_rows(w, tile_cols=None):
    k, n = w.shape
    rows = min(k, PACK_BLOCK_ROWS)
    assert k % rows == 0 and rows % (2 * PACKED_ROWS) == 0
    if tile_cols is None:
        out_spec = pl.BlockSpec((rows // 2, n), lambda i: (i, 0))
        out_shape = jax.ShapeDtypeStruct((k // 2, n), jnp.uint32)
    else:
        assert n % tile_cols == 0
        out_spec = pl.BlockSpec((n // tile_cols, rows // 2, tile_cols), lambda i: (0, i, 0))
        out_shape = jax.ShapeDtypeStruct((n // tile_cols, k // 2, tile_cols), jnp.uint32)
    return pl.pallas_call(
        _pack_rows_kernel,
        grid=(k // rows,),
        in_specs=[pl.BlockSpec((rows, n), lambda i: (i, 0))],
        out_specs=out_spec,
        out_shape=out_shape,
        compiler_params=pltpu.CompilerParams(vmem_limit_bytes=PACK_VMEM_LIMIT_BYTES),
    )(w)


def _decay(vec_ref):
    lam = vec_ref[V_LAM:V_LAM + 1, :]
    return RG_C * (jnp.maximum(-lam, 0.0) + jnp.log1p(jnp.exp(-jnp.abs(lam))))


def _row_sum(parts):
    s = parts[0]
    for p in parts[1:]:
        s = s + p
    return jnp.sum(s, axis=-1, keepdims=True)


def _load_rows(ref, c, R, B):
    if len(ref.shape) == 2:
        return ref[pl.ds(pl.multiple_of(c * R, R), R), :]
    t0 = pl.multiple_of(c * (R // B), R // B)
    xb = ref[:, pl.ds(t0, R // B), :]
    return jnp.swapaxes(xb, 0, 1).reshape(R, xb.shape[-1])


def _store_rows(ref, c, R, B, val):
    if len(ref.shape) == 2:
        ref[pl.ds(pl.multiple_of(c * R, R), R), :] = val
    else:
        t0 = pl.multiple_of(c * (R // B), R // B)
        ref[:, pl.ds(t0, R // B), :] = jnp.swapaxes(val.reshape(R // B, B, val.shape[-1]), 0, 1)


def _chunk_body(c, *, R, B, decay, x_ref, y_ref, vec_ref, cvw_ref, w_in_ref, wg_ref, pw2_ref, w_out_ref,
                x_scr, xn_scr, cv_scr, xr_ext, h_scr, v_ext, vb_ext):
    chunk_t = R // B
    r0 = pl.multiple_of(c * R, R)
    n_ptiles = R // PACKED_ROWS
    pstep = B // PACKED_ROWS

    def vec(row, cols):
        return vec_ref[row:row + 1, cols]

    x = _load_rows(x_ref, c, R, B)
    x_scr[...] = x
    xn = x * jax.lax.rsqrt(jnp.mean(x * x, axis=-1, keepdims=True) + EPS) * vec_ref[V_PRE_G:V_PRE_G + 1, :]
    xn_scr[...] = xn.astype(jnp.bfloat16)

    def in_proj_tile(sec, g):
        return jnp.dot(xn_scr[...], _bf16_rows(w_in_ref[sec * N_COL_TILES + g]), preferred_element_type=jnp.float32)

    for g in range(N_COL_TILES):
        v = in_proj_tile(SEC_CVV, g) * jax.nn.sigmoid(in_proj_tile(SEC_CVG, g))
        vb = v.astype(jnp.bfloat16)
        for half in range(LANES_PER_TILE):
            j = g * LANES_PER_TILE + half
            lcols = slice(half * LANES, (half + 1) * LANES)
            v_ext[j, pl.ds(pl.multiple_of(CV_HIST * B + r0, SUBLANES), R), :] = v[:, lcols]
            vb_ext[j, pl.ds(pl.multiple_of(CV_HIST * B + r0, PACKED_ROWS), R), :] = vb[:, lcols]

    def conv_lane_group(j, carry):
        acc = [jnp.broadcast_to(cvw_ref[j, CV_CONV:CV_CONV + 1, :], (PACKED_ROWS, LANES))] * n_ptiles
        for k0 in range(0, CV_CONV, CONV_TAP_GROUP):
            taps = range(k0, min(k0 + CONV_TAP_GROUP, CV_CONV))
            w = {k: jnp.broadcast_to(cvw_ref[j, k:k + 1, :], (PACKED_ROWS, LANES)).astype(jnp.bfloat16)
                 for k in taps}
            for i in range(n_ptiles):
                part = None
                for k in taps:
                    row = pl.multiple_of(r0 + (i + pstep * k) * PACKED_ROWS, PACKED_ROWS)
                    prod = w[k] * vb_ext[j, pl.ds(row, PACKED_ROWS), :]
                    part = prod if part is None else part + prod
                acc[i] = acc[i] + part.astype(jnp.float32)
        cv_scr[j] = jnp.concatenate(acc, axis=0)
        return carry

    jax.lax.fori_loop(0, N_LANE_GROUPS, conv_lane_group, 0)

    cv = [cv_scr[j] for j in range(N_LANE_GROUPS)]
    inv_d = 1.0 / D_CV
    mu = _row_sum(cv) * inv_d
    cvc = [p - mu for p in cv]
    var = _row_sum([p * p for p in cvc]) * inv_d
    rs = jax.lax.rsqrt(var + EPS)
    s_act = []
    for j in range(N_LANE_GROUPS):
        lcols = slice(j * LANES, (j + 1) * LANES)
        ln = cvc[j] * rs * vec(V_LN_G, lcols) + vec(V_LN_B, lcols)
        s_act.append(_silu(ln).astype(jnp.bfloat16))
    s_act = jnp.concatenate(s_act, axis=-1)

    for g in range(N_COL_TILES):
        u = in_proj_tile(SEC_XR, g)
        for half in range(LANES_PER_TILE):
            xr_ext[g * LANES_PER_TILE + half, pl.ds(pl.multiple_of(RG_HIST * B + r0, 8), R), :] = (
                u[:, half * LANES:(half + 1) * LANES])
    y_r = []
    for g in range(N_COL_TILES):
        cols = slice(g * MXU_TILE, (g + 1) * MXU_TILE)
        sgr = _silu(in_proj_tile(SEC_GR, g))
        halves = []
        for half in range(LANES_PER_TILE):
            j = g * LANES_PER_TILE + half
            lcols = slice(j * LANES, (j + 1) * LANES)
            acc = vec(V_RG_CB, lcols)
            for k in range(RG_CONV):
                tap = xr_ext[j, pl.ds(pl.multiple_of(k * B + r0, 8), R), :]
                acc = acc + vec(V_RG_CW + k, lcols) * tap
            halves.append(acc)
        xc = jnp.concatenate(halves, axis=-1)
        z = jnp.dot(xc.astype(jnp.bfloat16), _bf16_rows(wg_ref[g]), preferred_element_type=jnp.float32)
        r_gate = jax.nn.sigmoid(z[:, :MXU_TILE] + vec(V_BA, cols))
        i_gate = jax.nn.sigmoid(z[:, MXU_TILE:] + vec(V_BX, cols))
        neg_log_a = r_gate * decay[:, cols]
        a = jnp.exp(-neg_log_a)
        one_minus_a2 = jnp.tanh(neg_log_a) * (a * a + 1.0)
        bx = jnp.sqrt(one_minus_a2) * (i_gate * xc)
        h = h_scr[:, cols]
        hs = []
        for s in range(chunk_t):
            h = a[s * B:(s + 1) * B, :] * h + bx[s * B:(s + 1) * B, :]
            hs.append(h)
        h_scr[:, cols] = h
        y_r.append((jnp.concatenate(hs, axis=0) * sgr).astype(jnp.bfloat16))
    y_r = jnp.concatenate(y_r, axis=-1)

    z_c = jnp.dot(s_act, _bf16_rows(pw2_ref[...]), preferred_element_type=jnp.float32)
    sgc = jnp.concatenate([_silu(in_proj_tile(SEC_GC, g)) for g in range(N_COL_TILES)], axis=-1)
    y_c = ((z_c + vec_ref[V_B2:V_B2 + 1, :]) * sgc).astype(jnp.bfloat16)
    y = jnp.dot(jnp.concatenate([y_r, y_c], axis=-1), _bf16_rows(w_out_ref[...]),
                preferred_element_type=jnp.float32)
    yn = y * jax.lax.rsqrt(jnp.mean(y * y, axis=-1, keepdims=True) + EPS) * vec_ref[V_POST_G:V_POST_G + 1, :]
    _store_rows(y_ref, c, R, B, x_scr[...] + yn)


def _layer_kernel(*refs, batch, block_t, chunk_t, n_steps, lead_t):
    if lead_t:
        x_ref, lead_ref, *refs = refs
    else:
        x_ref, *refs = refs
    (h0_ref, rgb0_ref, cvb0_ref, vec_ref, cvw_ref, w_in_ref, wg_ref, pw2_ref, w_out_ref,
     y_ref, h_out_ref, rgb_out_ref, cvb_out_ref, x_scr, xn_scr, cv_scr, xr_ext, h_scr, v_ext, vb_ext) = refs
    B = batch
    R = chunk_t * B
    blk = block_t * B
    step = pl.program_id(0)

    @pl.when(step == 0)
    def _init():
        for j in range(N_LANE_GROUPS):
            lcols = slice(j * LANES, (j + 1) * LANES)
            xr_ext[j, 0:RG_HIST * B, :] = rgb0_ref[:, lcols]
            v_ext[j, 0:CV_HIST * B, :] = cvb0_ref[:, lcols]
            vb_ext[j, 0:CV_HIST * B, :] = cvb0_ref[:, lcols].astype(vb_ext.dtype)
        h_scr[...] = h0_ref[...]

    decay = _decay(vec_ref)

    def run_chunks(src_ref, n_chunks):
        def chunk(c, carry):
            _chunk_body(c, R=R, B=B, decay=decay, x_ref=src_ref, y_ref=y_ref, vec_ref=vec_ref,
                        cvw_ref=cvw_ref, w_in_ref=w_in_ref, wg_ref=wg_ref, pw2_ref=pw2_ref, w_out_ref=w_out_ref,
                        x_scr=x_scr, xn_scr=xn_scr, cv_scr=cv_scr, xr_ext=xr_ext, h_scr=h_scr, v_ext=v_ext,
                        vb_ext=vb_ext)
            return carry

        jax.lax.fori_loop(0, n_chunks, chunk, 0)

    def carry_hist(rows):
        xr_ext[:, 0:RG_HIST * B, :] = xr_ext[:, rows:rows + RG_HIST * B, :]
        v_ext[:, 0:CV_HIST * B, :] = v_ext[:, rows:rows + CV_HIST * B, :]
        vb_ext[:, 0:CV_HIST * B, :] = vb_ext[:, rows:rows + CV_HIST * B, :]

    def block_step():
        run_chunks(x_ref, block_t // chunk_t)

        @pl.when(step == n_steps - 1)
        def _emit_state():
            h_out_ref[...] = h_scr[...]
            for j in range(N_LANE_GROUPS):
                lcols = slice(j * LANES, (j + 1) * LANES)
                rgb_out_ref[:, lcols] = xr_ext[j, blk:blk + RG_HIST * B, :]
                cvb_out_ref[:, lcols] = v_ext[j, blk:blk + CV_HIST * B, :]

        if n_steps > 1:
            pl.when(step < n_steps - 1)(functools.partial(carry_hist, blk))

    if lead_t:
        @pl.when(step == 0)
        def _lead():
            run_chunks(lead_ref, lead_t // chunk_t)
            carry_hist(lead_t * B)

        pl.when(step > 0)(block_step)
    else:
        block_step()


def _to_slabs(s):
    r, c = s.shape
    return s.reshape(r, c // LANES, LANES).transpose(1, 0, 2)


def _stream_layer(x_rows, h0, rgb0, cvb0, weights, *, batch, n_t, block_t, chunk_rows=CHUNK_ROWS, lead_rows=None):
    B = batch
    assert B % PACKED_ROWS == 0 and chunk_rows % B == 0
    chunk_t = chunk_rows // B
    assert n_t % block_t == 0 and block_t % chunk_t == 0
    lead_t = 0 if lead_rows is None else lead_rows.shape[0] // B
    assert lead_t % chunk_t == 0 and lead_t <= block_t
    lead = 1 if lead_t else 0
    n_steps = n_t // block_t + lead
    assert n_steps == 1 or block_t >= CV_HIST
    blk = block_t * B
    f32 = jnp.float32

    def full(a):
        return pl.BlockSpec(a.shape, lambda i: (0,) * a.ndim)

    kern = functools.partial(_layer_kernel, batch=B, block_t=block_t, chunk_t=chunk_t, n_steps=n_steps,
                             lead_t=lead_t)
    consts = (() if lead_rows is None else (lead_rows,)) + (h0, rgb0, cvb0) + tuple(weights)

    def block_index(i):
        return jnp.maximum(i - lead, 0)

    if x_rows.ndim == 2:
        xy_spec = pl.BlockSpec((blk, D_MODEL), lambda i: (block_index(i), 0))
    else:
        assert x_rows.shape == (B, n_t, D_MODEL)
        xy_spec = pl.BlockSpec((B, block_t, D_MODEL), lambda i: (0, block_index(i), 0))
    return pl.pallas_call(
        kern,
        grid=(n_steps,),
        in_specs=[xy_spec] + [full(a) for a in consts],
        out_specs=[xy_spec,
                   pl.BlockSpec((B, D_RG), lambda i: (0, 0)),
                   pl.BlockSpec((RG_HIST * B, D_RG), lambda i: (0, 0)),
                   pl.BlockSpec((CV_HIST * B, D_CV), lambda i: (0, 0))],
        out_shape=[jax.ShapeDtypeStruct(x_rows.shape, f32),
                   jax.ShapeDtypeStruct((B, D_RG), f32),
                   jax.ShapeDtypeStruct((RG_HIST * B, D_RG), f32),
                   jax.ShapeDtypeStruct((CV_HIST * B, D_CV), f32)],
        scratch_shapes=[pltpu.VMEM((chunk_rows, D_MODEL), f32),
                        pltpu.VMEM((chunk_rows, D_MODEL), jnp.bfloat16),
                        pltpu.VMEM((N_LANE_GROUPS, chunk_rows, LANES), f32),
                        pltpu.VMEM((N_LANE_GROUPS, (RG_HIST + block_t) * B, LANES), f32),
                        pltpu.VMEM((B, D_RG), f32),
                        pltpu.VMEM((N_LANE_GROUPS, (CV_HIST + block_t) * B, LANES), f32),
                        pltpu.VMEM((N_LANE_GROUPS, (CV_HIST + block_t) * B, LANES), jnp.bfloat16)],
        compiler_params=pltpu.CompilerParams(dimension_semantics=("arbitrary",),
                                             vmem_limit_bytes=VMEM_LIMIT_BYTES),
    )(x_rows, *consts)


def _block_diag_gates(wa, wx):
    hpg = MXU_TILE // RG_HEAD_DIM
    eye = jnp.eye(hpg, dtype=wa.dtype)

    def bd(w):
        w4 = w.reshape(N_COL_TILES, hpg, RG_HEAD_DIM, RG_HEAD_DIM)
        return jnp.einsum('ghij,hk->ghikj', w4, eye).reshape(N_COL_TILES, MXU_TILE, MXU_TILE)

    return jnp.concatenate([bd(wa), bd(wx)], axis=-1)


def _to_rows(x):
    b, t, c = x.shape
    return x.transpose(1, 0, 2).reshape(t * b, c)


def _from_rows(x, b):
    tb, c = x.shape
    return x.reshape(tb // b, b, c).transpose(1, 0, 2)


def kernel(x_prompt, x_sample, state_rg_h, state_rg_conv, state_cv_conv, meta_tokens, pre_norm_g, w_in,
           rg_conv_w, rg_conv_b, rg_wa, rg_ba, rg_wx, rg_bx, rg_lambda, cv_conv_w, cv_conv_b, cv_ln_g,
           cv_ln_b, cv_w_pw2, cv_b_pw2, w_out, post_norm_g):
    assert w_in.shape[0] == 1, "single layer"
    f32 = jnp.float32
    bsz, seq, _ = x_prompt.shape
    dbsz, dseq, _ = x_sample.shape

    rows = [pre_norm_g[0], rg_conv_b[0], rg_ba[0], rg_bx[0], rg_lambda[0], cv_ln_g[0], cv_ln_b[0],
            cv_b_pw2[0], post_norm_g[0]]
    vecs = jnp.concatenate([jnp.stack(rows).astype(f32), rg_conv_w[0].astype(f32),
                            jnp.zeros((N_VEC_ROWS - len(rows) - RG_CONV, D_MODEL), f32)], axis=0)
    cvw = _to_slabs(jnp.concatenate([cv_conv_w[0].astype(f32), cv_conv_b[0].astype(f32)[None]], axis=0))
    wg = _block_diag_gates(rg_wa[0], rg_wx[0])
    wg = _pack_rows(wg.reshape(N_COL_TILES * MXU_TILE, 2 * MXU_TILE)).reshape(N_COL_TILES, MXU_TILE // 2, 2 * MXU_TILE)
    w_in_tiles = _pack_rows(w_in[0], tile_cols=MXU_TILE)
    weights = (vecs, cvw, w_in_tiles, wg, _pack_rows(cv_w_pw2[0]), _pack_rows(w_out[0]))

    xm = jnp.broadcast_to(meta_tokens.astype(f32)[:, None, :], (N_META, bsz, D_MODEL)).reshape(N_META * bsz, D_MODEL)
    yp, h_p, rgb_p, cvb_p = _stream_layer(
        x_prompt, jnp.zeros((bsz, D_RG), f32), jnp.zeros((RG_HIST * bsz, D_RG), f32),
        jnp.zeros((CV_HIST * bsz, D_CV), f32), weights, batch=bsz, n_t=seq, block_t=32, chunk_rows=256,
        lead_rows=xm)

    ys, h_s, rgb_s, cvb_s = _stream_layer(
        x_sample, state_rg_h[0], _to_rows(state_rg_conv[0]), _to_rows(state_cv_conv[0]),
        weights, batch=dbsz, n_t=dseq, block_t=dseq, chunk_rows=256)

    def conv_state(s, b):
        return _from_rows(s, b)[None]

    return (yp, ys,
            h_p[None], conv_state(rgb_p, bsz), conv_state(cvb_p, bsz),
            h_s[None], conv_state(rgb_s, dbsz), conv_state(cvb_s, dbsz))
```

```python
import functools

import jax
import jax.numpy as jnp
from jax.experimental import pallas as pl
from jax.experimental.pallas import tpu as pltpu

D_MODEL = 1024
N_META = 16
D_RG = 1024
N_RG_HEADS = 16
RG_HEAD_DIM = D_RG // N_RG_HEADS
RG_CONV = 4
RG_C = 8.0
D_CV = 1024
CV_CONV = 31
EPS = 1e-6

RG_HIST = RG_CONV - 1
CV_HIST = CV_CONV - 1
SUBLANES, LANES = 8, 128
PACKED_ROWS = 16
PACK_STEPS = 4
PACK_VMEM_LIMIT_BYTES = 32 * 1024 * 1024
MXU_TILE = 256
N_COL_TILES = D_RG // MXU_TILE
LANES_PER_TILE = MXU_TILE // LANES
N_LANE_GROUPS = D_CV // LANES
N_IN_SECTIONS = 5
SEC_XR, SEC_GR, SEC_CVV, SEC_CVG, SEC_GC = range(N_IN_SECTIONS)
CHUNK_ROWS = 128
CONV_TAP_GROUP = 4
VMEM_LIMIT_BYTES = 58 * 1024 * 1024

V_PRE_G, V_RG_CB, V_BA, V_BX, V_LAM, V_LN_G, V_LN_B, V_B2, V_POST_G, V_RG_CW = range(10)
N_VEC_ROWS = 16
CVW_ROWS = 32


def _silu(x):
    return x * jax.nn.sigmoid(x)


def _bf16_rows(w_u32):
    return pltpu.bitcast(w_u32, jnp.bfloat16)


def _pack_rows_kernel(*refs):
    n = len(refs) // 2
    for w_ref, o_ref in zip(refs[:n], refs[n:]):
        packed = pltpu.bitcast(w_ref[...].astype(jnp.bfloat16), jnp.uint32)
        if len(o_ref.shape) == 2:
            o_ref[...] = packed
        else:
            tile_cols = o_ref.shape[-1]
            for t in range(o_ref.shape[0]):
                o_ref[t] = packed[:, t * tile_cols:(t + 1) * tile_cols]


def _pack_rows(ws, tile_cols):
    in_specs, out_specs, out_shapes = [], [], []
    for w, tc in zip(ws, tile_cols):
        k, n = w.shape
        rows = k // PACK_STEPS
        assert k % PACK_STEPS == 0 and rows % (2 * PACKED_ROWS) == 0
        in_specs.append(pl.BlockSpec((rows, n), lambda i: (i, 0)))
        if tc is None:
            out_specs.append(pl.BlockSpec((rows // 2, n), lambda i: (i, 0)))
            out_shapes.append(jax.ShapeDtypeStruct((k // 2, n), jnp.uint32))
        else:
            assert n % tc == 0
            out_specs.append(pl.BlockSpec((n // tc, rows // 2, tc), lambda i: (0, i, 0)))
            out_shapes.append(jax.ShapeDtypeStruct((n // tc, k // 2, tc), jnp.uint32))
    return pl.pallas_call(
        _pack_rows_kernel,
        grid=(PACK_STEPS,),
        in_specs=in_specs,
        out_specs=out_specs,
        out_shape=out_shapes,
        compiler_params=pltpu.CompilerParams(vmem_limit_bytes=PACK_VMEM_LIMIT_BYTES),
    )(*ws)


def _decay(vec_ref):
    lam = vec_ref[V_LAM:V_LAM + 1, :]
    return RG_C * (jnp.maximum(-lam, 0.0) + jnp.log1p(jnp.exp(-jnp.abs(lam))))


def _row_sum(parts):
    s = parts[0]
    for p in parts[1:]:
        s = s + p
    return jnp.sum(s, axis=-1, keepdims=True)


def _load_rows(ref, c, R, B):
    if len(ref.shape) == 2:
        return ref[pl.ds(pl.multiple_of(c * R, R), R), :]
    t0 = pl.multiple_of(c * (R // B), R // B)
    xb = ref[:, pl.ds(t0, R // B), :]
    return jnp.swapaxes(xb, 0, 1).reshape(R, xb.shape[-1])


def _store_rows(ref, c, R, B, val):
    if len(ref.shape) == 2:
        ref[pl.ds(pl.multiple_of(c * R, R), R), :] = val
    else:
        t0 = pl.multiple_of(c * (R // B), R // B)
        ref[:, pl.ds(t0, R // B), :] = jnp.swapaxes(val.reshape(R // B, B, val.shape[-1]), 0, 1)


def _chunk_body(c, *, R, B, decay, x_ref, y_ref, vec_ref, cvw_ref, w_in_ref, wg_ref, pw2_ref, w_out_ref,
                x_scr, xn_scr, cv_scr, xr_ext, h_scr, v_ext, vb_ext):
    chunk_t = R // B
    r0 = pl.multiple_of(c * R, R)
    n_ptiles = R // PACKED_ROWS
    pstep = B // PACKED_ROWS

    def vec(row, cols):
        return vec_ref[row:row + 1, cols]

    x = _load_rows(x_ref, c, R, B)
    x_scr[...] = x
    xn = x * jax.lax.rsqrt(jnp.mean(x * x, axis=-1, keepdims=True) + EPS) * vec_ref[V_PRE_G:V_PRE_G + 1, :]
    xn_scr[...] = xn.astype(jnp.bfloat16)

    def in_proj_tile(sec, g):
        return jnp.dot(xn_scr[...], _bf16_rows(w_in_ref[sec * N_COL_TILES + g]), preferred_element_type=jnp.float32)

    for g in range(N_COL_TILES):
        v = in_proj_tile(SEC_CVV, g) * jax.nn.sigmoid(in_proj_tile(SEC_CVG, g))
        vb = v.astype(jnp.bfloat16)
        for half in range(LANES_PER_TILE):
            j = g * LANES_PER_TILE + half
            lcols = slice(half * LANES, (half + 1) * LANES)
            v_ext[j, pl.ds(pl.multiple_of(CV_HIST * B + r0, SUBLANES), R), :] = v[:, lcols]
            vb_ext[j, pl.ds(pl.multiple_of(CV_HIST * B + r0, PACKED_ROWS), R), :] = vb[:, lcols]

    def conv_lane_group(j, carry):
        acc = [jnp.broadcast_to(cvw_ref[j, CV_CONV:CV_CONV + 1, :], (PACKED_ROWS, LANES))] * n_ptiles
        for k0 in range(0, CV_CONV, CONV_TAP_GROUP):
            taps = range(k0, min(k0 + CONV_TAP_GROUP, CV_CONV))
            w = {k: jnp.broadcast_to(cvw_ref[j, k:k + 1, :], (PACKED_ROWS, LANES)).astype(jnp.bfloat16)
                 for k in taps}
            for i in range(n_ptiles):
                part = None
                for k in taps:
                    row = pl.multiple_of(r0 + (i + pstep * k) * PACKED_ROWS, PACKED_ROWS)
                    prod = w[k] * vb_ext[j, pl.ds(row, PACKED_ROWS), :]
                    part = prod if part is None else part + prod
                acc[i] = acc[i] + part.astype(jnp.float32)
        cv_scr[j] = jnp.concatenate(acc, axis=0)
        return carry

    jax.lax.fori_loop(0, N_LANE_GROUPS, conv_lane_group, 0)

    cv = [cv_scr[j] for j in range(N_LANE_GROUPS)]
    inv_d = 1.0 / D_CV
    mu = _row_sum(cv) * inv_d
    cvc = [p - mu for p in cv]
    var = _row_sum([p * p for p in cvc]) * inv_d
    rs = jax.lax.rsqrt(var + EPS)
    s_act = []
    for j in range(N_LANE_GROUPS):
        lcols = slice(j * LANES, (j + 1) * LANES)
        ln = cvc[j] * rs * vec(V_LN_G, lcols) + vec(V_LN_B, lcols)
        s_act.append(_silu(ln).astype(jnp.bfloat16))
    s_act = jnp.concatenate(s_act, axis=-1)

    for g in range(N_COL_TILES):
        u = in_proj_tile(SEC_XR, g)
        for half in range(LANES_PER_TILE):
            xr_ext[g * LANES_PER_TILE + half, pl.ds(pl.multiple_of(RG_HIST * B + r0, 8), R), :] = (
                u[:, half * LANES:(half + 1) * LANES])
    y_r = []
    for g in range(N_COL_TILES):
        cols = slice(g * MXU_TILE, (g + 1) * MXU_TILE)
        sgr = _silu(in_proj_tile(SEC_GR, g))
        halves = []
        for half in range(LANES_PER_TILE):
            j = g * LANES_PER_TILE + half
            lcols = slice(j * LANES, (j + 1) * LANES)
            acc = vec(V_RG_CB, lcols)
            for k in range(RG_CONV):
                tap = xr_ext[j, pl.ds(pl.multiple_of(k * B + r0, 8), R), :]
                acc = acc + vec(V_RG_CW + k, lcols) * tap
            halves.append(acc)
        xc = jnp.concatenate(halves, axis=-1)
        z = jnp.dot(xc.astype(jnp.bfloat16), _bf16_rows(wg_ref[g]), preferred_element_type=jnp.float32)
        r_gate = jax.nn.sigmoid(z[:, :MXU_TILE] + vec(V_BA, cols))
        i_gate = jax.nn.sigmoid(z[:, MXU_TILE:] + vec(V_BX, cols))
        neg_log_a = r_gate * decay[:, cols]
        a = jnp.exp(-neg_log_a)
        one_minus_a2 = jnp.tanh(neg_log_a) * (a * a + 1.0)
        bx = jnp.sqrt(one_minus_a2) * (i_gate * xc)
        h = h_scr[:, cols]
        hs = []
        for s in range(chunk_t):
            h = a[s * B:(s + 1) * B, :] * h + bx[s * B:(s + 1) * B, :]
            hs.append(h)
        h_scr[:, cols] = h
        y_r.append((jnp.concatenate(hs, axis=0) * sgr).astype(jnp.bfloat16))
    y_r = jnp.concatenate(y_r, axis=-1)

    z_c = jnp.dot(s_act, _bf16_rows(pw2_ref[...]), preferred_element_type=jnp.float32)
    sgc = jnp.concatenate([_silu(in_proj_tile(SEC_GC, g)) for g in range(N_COL_TILES)], axis=-1)
    y_c = ((z_c + vec_ref[V_B2:V_B2 + 1, :]) * sgc).astype(jnp.bfloat16)
    y = jnp.dot(jnp.concatenate([y_r, y_c], axis=-1), _bf16_rows(w_out_ref[...]),
                preferred_element_type=jnp.float32)
    yn = y * jax.lax.rsqrt(jnp.mean(y * y, axis=-1, keepdims=True) + EPS) * vec_ref[V_POST_G:V_POST_G + 1, :]
    _store_rows(y_ref, c, R, B, x_scr[...] + yn)


def _layer_kernel(*refs, batch, block_t, chunk_t, n_steps, lead_t):
    if lead_t:
        x_ref, lead_ref, *refs = refs
    else:
        x_ref, *refs = refs
    (h0_ref, rgb0_ref, cvb0_ref, vec_ref, cvw_ref, w_in_ref, wg_ref, pw2_ref, w_out_ref,
     y_ref, h_out_ref, rgb_out_ref, cvb_out_ref, x_scr, xn_scr, cv_scr, xr_ext, h_scr, v_ext, vb_ext) = refs
    B = batch
    R = chunk_t * B
    blk = block_t * B
    step = pl.program_id(0)

    @pl.when(step == 0)
    def _init():
        for j in range(N_LANE_GROUPS):
            lcols = slice(j * LANES, (j + 1) * LANES)
            xr_ext[j, 0:RG_HIST * B, :] = rgb0_ref[:, lcols]
            v_ext[j, 0:CV_HIST * B, :] = cvb0_ref[:, lcols]
            vb_ext[j, 0:CV_HIST * B, :] = cvb0_ref[:, lcols].astype(vb_ext.dtype)
        h_scr[...] = h0_ref[...]

    decay = _decay(vec_ref)

    def run_chunks(src_ref, n_chunks):
        def chunk(c, carry):
            _chunk_body(c, R=R, B=B, decay=decay, x_ref=src_ref, y_ref=y_ref, vec_ref=vec_ref,
                        cvw_ref=cvw_ref, w_in_ref=w_in_ref, wg_ref=wg_ref, pw2_ref=pw2_ref, w_out_ref=w_out_ref,
                        x_scr=x_scr, xn_scr=xn_scr, cv_scr=cv_scr, xr_ext=xr_ext, h_scr=h_scr, v_ext=v_ext,
                        vb_ext=vb_ext)
            return carry

        jax.lax.fori_loop(0, n_chunks, chunk, 0)

    def carry_hist(rows):
        xr_ext[:, 0:RG_HIST * B, :] = xr_ext[:, rows:rows + RG_HIST * B, :]
        v_ext[:, 0:CV_HIST * B, :] = v_ext[:, rows:rows + CV_HIST * B, :]
        vb_ext[:, 0:CV_HIST * B, :] = vb_ext[:, rows:rows + CV_HIST * B, :]

    def block_step():
        run_chunks(x_ref, block_t // chunk_t)

        @pl.when(step == n_steps - 1)
        def _emit_state():
            h_out_ref[...] = h_scr[...]
            for j in range(N_LANE_GROUPS):
                lcols = slice(j * LANES, (j + 1) * LANES)
                rgb_out_ref[:, lcols] = xr_ext[j, blk:blk + RG_HIST * B, :]
                cvb_out_ref[:, lcols] = v_ext[j, blk:blk + CV_HIST * B, :]

        if n_steps > 1:
            pl.when(step < n_steps - 1)(functools.partial(carry_hist, blk))

    if lead_t:
        @pl.when(step == 0)
        def _lead():
            run_chunks(lead_ref, lead_t // chunk_t)
            carry_hist(lead_t * B)

        pl.when(step > 0)(block_step)
    else:
        block_step()


def _to_slabs(s):
    r, c = s.shape
    return s.reshape(r, c // LANES, LANES).transpose(1, 0, 2)


def _stream_layer(x_rows, h0, rgb0, cvb0, weights, *, batch, n_t, block_t, chunk_rows=CHUNK_ROWS, lead_rows=None):
    B = batch
    assert B % PACKED_ROWS == 0 and chunk_rows % B == 0
    chunk_t = chunk_rows // B
    assert n_t % block_t == 0 and block_t % chunk_t == 0
    lead_t = 0 if lead_rows is None else lead_rows.shape[0] // B
    assert lead_t % chunk_t == 0 and lead_t <= block_t
    lead = 1 if lead_t else 0
    n_steps = n_t // block_t + lead
    assert n_steps == 1 or block_t >= CV_HIST
    blk = block_t * B
    f32 = jnp.float32

    def full(a):
        return pl.BlockSpec(a.shape, lambda i: (0,) * a.ndim)

    kern = functools.partial(_layer_kernel, batch=B, block_t=block_t, chunk_t=chunk_t, n_steps=n_steps,
                             lead_t=lead_t)
    consts = (() if lead_rows is None else (lead_rows,)) + (h0, rgb0, cvb0) + tuple(weights)

    def block_index(i):
        return jnp.maximum(i - lead, 0)

    if x_rows.ndim == 2:
        xy_spec = pl.BlockSpec((blk, D_MODEL), lambda i: (block_index(i), 0))
    else:
        assert x_rows.shape == (B, n_t, D_MODEL)
        xy_spec = pl.BlockSpec((B, block_t, D_MODEL), lambda i: (0, block_index(i), 0))
    return pl.pallas_call(
        kern,
        grid=(n_steps,),
        in_specs=[xy_spec] + [full(a) for a in consts],
        out_specs=[xy_spec,
                   pl.BlockSpec((B, D_RG), lambda i: (0, 0)),
                   pl.BlockSpec((RG_HIST * B, D_RG), lambda i: (0, 0)),
                   pl.BlockSpec((CV_HIST * B, D_CV), lambda i: (0, 0))],
        out_shape=[jax.ShapeDtypeStruct(x_rows.shape, f32),
                   jax.ShapeDtypeStruct((B, D_RG), f32),
                   jax.ShapeDtypeStruct((RG_HIST * B, D_RG), f32),
                   jax.ShapeDtypeStruct((CV_HIST * B, D_CV), f32)],
        scratch_shapes=[pltpu.VMEM((chunk_rows, D_MODEL), f32),
                        pltpu.VMEM((chunk_rows, D_MODEL), jnp.bfloat16),
                        pltpu.VMEM((N_LANE_GROUPS, chunk_rows, LANES), f32),
                        pltpu.VMEM((N_LANE_GROUPS, (RG_HIST + block_t) * B, LANES), f32),
                        pltpu.VMEM((B, D_RG), f32),
                        pltpu.VMEM((N_LANE_GROUPS, (CV_HIST + block_t) * B, LANES), f32),
                        pltpu.VMEM((N_LANE_GROUPS, (CV_HIST + block_t) * B, LANES), jnp.bfloat16)],
        compiler_params=pltpu.CompilerParams(dimension_semantics=("arbitrary",),
                                             vmem_limit_bytes=VMEM_LIMIT_BYTES),
    )(x_rows, *consts)


def _block_diag_gates(wa, wx):
    hpg = MXU_TILE // RG_HEAD_DIM
    eye = jnp.eye(hpg, dtype=wa.dtype)

    def bd(w):
        w4 = w.reshape(N_COL_TILES, hpg, RG_HEAD_DIM, RG_HEAD_DIM)
        return jnp.einsum('ghij,hk->ghikj', w4, eye).reshape(N_COL_TILES, MXU_TILE, MXU_TILE)

    return jnp.concatenate([bd(wa), bd(wx)], axis=-1)


def _to_rows(x):
    b, t, c = x.shape
    return x.transpose(1, 0, 2).reshape(t * b, c)


def _from_rows(x, b):
    tb, c = x.shape
    return x.reshape(tb // b, b, c).transpose(1, 0, 2)


def kernel(x_prompt, x_sample, state_rg_h, state_rg_conv, state_cv_conv, meta_tokens, pre_norm_g, w_in,
           rg_conv_w, rg_conv_b, rg_wa, rg_ba, rg_wx, rg_bx, rg_lambda, cv_conv_w, cv_conv_b, cv_ln_g,
           cv_ln_b, cv_w_pw2, cv_b_pw2, w_out, post_norm_g):
    assert w_in.shape[0] == 1, "single layer"
    f32 = jnp.float32
    bsz, seq, _ = x_prompt.shape
    dbsz, dseq, _ = x_sample.shape

    rows = [pre_norm_g[0], rg_conv_b[0], rg_ba[0], rg_bx[0], rg_lambda[0], cv_ln_g[0], cv_ln_b[0],
            cv_b_pw2[0], post_norm_g[0]]
    vecs = jnp.concatenate([jnp.stack(rows).astype(f32), rg_conv_w[0].astype(f32),
                            jnp.zeros((N_VEC_ROWS - len(rows) - RG_CONV, D_MODEL), f32)], axis=0)
    cvw = _to_slabs(jnp.concatenate([cv_conv_w[0].astype(f32), cv_conv_b[0].astype(f32)[None]], axis=0))
    wg = _block_diag_gates(rg_wa[0], rg_wx[0])
    w_in_tiles, wg, pw2, w_o = _pack_rows(
        (w_in[0], wg.reshape(N_COL_TILES * MXU_TILE, 2 * MXU_TILE), cv_w_pw2[0], w_out[0]),
        (MXU_TILE, None, None, None))
    wg = wg.reshape(N_COL_TILES, MXU_TILE // 2, 2 * MXU_TILE)
    weights = (vecs, cvw, w_in_tiles, wg, pw2, w_o)

    xm = jnp.broadcast_to(meta_tokens.astype(f32)[:, None, :], (N_META, bsz, D_MODEL)).reshape(N_META * bsz, D_MODEL)
    yp, h_p, rgb_p, cvb_p = _stream_layer(
        x_prompt, jnp.zeros((bsz, D_RG), f32), jnp.zeros((RG_HIST * bsz, D_RG), f32),
        jnp.zeros((CV_HIST * bsz, D_CV), f32), weights, batch=bsz, n_t=seq, block_t=32, chunk_rows=256,
        lead_rows=xm)

    ys, h_s, rgb_s, cvb_s = _stream_layer(
        x_sample, state_rg_h[0], _to_rows(state_rg_conv[0]), _to_rows(state_cv_conv[0]),
        weights, batch=dbsz, n_t=dseq, block_t=dseq, chunk_rows=256)

    def conv_state(s, b):
        return _from_rows(s, b)[None]

    return (yp, ys,
            h_p[None], conv_state(rgb_p, bsz), conv_state(cvb_p, bsz),
            h_s[None], conv_state(rgb_s, dbsz), conv_state(cvb_s, dbsz))
```
